```python
import math
import jax, jax.numpy as jnp
from jax import lax
import numpy as np

D_MODEL = 1024
BATCH = 4
SEQ = 4096
DEPTH = 2

HEAD_DIM = 64
N_ATTN_HEADS = 8
ATTN_WIDTH = N_ATTN_HEADS * HEAD_DIM
DILATED_BRANCHES = ((128, 1), (512, 4), (2048, 16))
N_SSD_HEADS = 4
SSD_HEAD_DIM = 64
SSD_WIDTH = N_SSD_HEADS * SSD_HEAD_DIM
SSD_GROUPS = 2
SSD_STATE = 128
CONV_WIDTH = 4
SSD_CHUNK = 128
CONV_CH = SSD_WIDTH + 2 * SSD_GROUPS * SSD_STATE
N_MEM_HEADS = 4
MEM_WIDTH = N_MEM_HEADS * HEAD_DIM
N_MEM = 256
MIX_WIDTH = ATTN_WIDTH + SSD_WIDTH + MEM_WIDTH
N_IN_COLS = 3 * ATTN_WIDTH + SSD_WIDTH + CONV_CH + N_SSD_HEADS + MEM_WIDTH
N_BUCKETS = 32
MAX_DISTANCE = 2048
N_EXPERTS = 16
N_EXPERT_GROUPS = 4
EXPERTS_PER_GROUP = N_EXPERTS // N_EXPERT_GROUPS
TOP_K = 2
D_EXPERT = 1024
ALPHA = (2 * DEPTH) ** 0.25
BETA = (8 * DEPTH) ** -0.25
LN_EPS = 1e-5
RMS_EPS = 1e-5

kernel_name = 'hybrid_dilated_ssd_memory_moe'


def _layer_norm(x, g, b):
    xf = x.astype(jnp.float32)
    mu = xf.mean(-1, keepdims=True)
    var = jnp.square(xf - mu).mean(-1, keepdims=True)
    return ((xf - mu) * lax.rsqrt(var + LN_EPS) * g + b).astype(x.dtype)


def _t5_bucket(dist):
    n = jnp.maximum(dist, 0)
    max_exact = N_BUCKETS // 2
    nf = jnp.maximum(n, max_exact).astype(jnp.float32)
    large = max_exact + (jnp.log(nf / max_exact) / math.log(MAX_DISTANCE / max_exact)
                         * (N_BUCKETS - max_exact)).astype(jnp.int32)
    large = jnp.minimum(large, N_BUCKETS - 1)
    return jnp.where(n < max_exact, n, large)


def _dilated_branch(q, k, v, rel_bias, window, dilation):
    b, s, h, e = q.shape
    steps_max = window // dilation
    blk = steps_max
    L = s // dilation
    nb = -(-L // blk)
    Lp = nb * blk

    def to_blocks(t):
        t = t.reshape(b, L, dilation, h, e).transpose(0, 2, 1, 3, 4)
        t = jnp.pad(t, ((0, 0), (0, 0), (0, Lp - L), (0, 0), (0, 0)))
        return t.reshape(b, dilation, nb, blk, h, e)

    def with_prev(t):
        prev = jnp.pad(t[:, :, :-1], ((0, 0), (0, 0), (1, 0), (0, 0), (0, 0), (0, 0)))
        return jnp.concatenate([prev, t], axis=3)

    qb = to_blocks(q)
    kk = with_prev(to_blocks(k))
    vv = with_prev(to_blocks(v))

    steps = jnp.arange(blk)[:, None] + blk - jnp.arange(2 * blk)[None, :]
    in_band = (steps >= 0) & (steps <= steps_max)
    has_prev = (jnp.arange(nb)[:, None, None] > 0) | (jnp.arange(2 * blk)[None, None, :] >= blk)
    valid = in_band[None] & has_prev
    bias = rel_bias[_t5_bucket(steps * dilation)].transpose(2, 0, 1).astype(jnp.float32)

    sc = jnp.einsum('brnqhe,brnkhe->brnhqk', qb, kk).astype(jnp.float32) + bias
    sc = jnp.where(valid[:, None], sc, -jnp.inf)
    m = sc.max(-1, keepdims=True)
    p = jnp.exp(sc - m)
    den = p.sum(-1, keepdims=True)
    o = jnp.einsum('brnhqk,brnkhe->brnqhe', p / den, vv.astype(jnp.float32))
    lse = (m + jnp.log(den))[..., 0]

    o = o.reshape(b, dilation, Lp, h, e)[:, :, :L].transpose(0, 2, 1, 3, 4).reshape(b, s, h, e)
    lse = lse.transpose(0, 1, 2, 4, 3).reshape(b, dilation, Lp, h)[:, :, :L]
    lse = lse.transpose(0, 2, 1, 3).reshape(b, s, h)
    return o, lse


def _dilated_attention(q, k, v, rel_bias):
    outs, lses = zip(*[_dilated_branch(q, k, v, rel_bias, w, d) for w, d in DILATED_BRANCHES])
    wts = jax.nn.softmax(jnp.stack(lses), axis=0)
    return jnp.einsum('ibsh,ibshe->bshe', wts, jnp.stack(outs))


def _causal_conv(u, w, bias):
    out = lax.conv_general_dilated(u, w[:, None, :], window_strides=(1,),
                                   padding=[(CONV_WIDTH - 1, 0)],
                                   dimension_numbers=('NWC', 'WIO', 'NWC'),
                                   feature_group_count=u.shape[-1])
    return out + bias


def _segsum(a):
    t = a.shape[-1]
    cs = jnp.cumsum(a, axis=-1)
    d = cs[..., :, None] - cs[..., None, :]
    return jnp.where(jnp.tril(jnp.ones((t, t), dtype=bool)), d, -jnp.inf)


def _ssd_chunked(xdt, adt, bm, cm):
    b, s, h, p = xdt.shape
    n = bm.shape[-1]
    c = s // SSD_CHUNK
    X = xdt.reshape(b, c, SSD_CHUNK, h, p)
    Bc = bm.reshape(b, c, SSD_CHUNK, h, n)
    Cc = cm.reshape(b, c, SSD_CHUNK, h, n)
    A = adt.reshape(b, c, SSD_CHUNK, h).transpose(0, 3, 1, 2)
    a_cs = jnp.cumsum(A, axis=-1)
    Lm = jnp.exp(_segsum(A))
    y_diag = jnp.einsum('bclhn,bcshn,bhcls,bcshp->bclhp', Cc, Bc, Lm, X)
    decay_states = jnp.exp(a_cs[..., -1:] - a_cs)
    states = jnp.einsum('bclhn,bhcl,bclhp->bchpn', Bc, decay_states, X)
    states = jnp.concatenate([jnp.zeros_like(states[:, :1]), states], axis=1)
    decay_chunk = jnp.exp(_segsum(jnp.pad(a_cs[..., -1], ((0, 0), (0, 0), (1, 0)))))
    states = jnp.einsum('bhzc,bchpn->bzhpn', decay_chunk, states)[:, :-1]
    y_off = jnp.einsum('bclhn,bchpn,bhcl->bclhp', Cc, states, jnp.exp(a_cs))
    return (y_diag + y_off).reshape(b, s, h, p)


def _ssd_mixer(z, xbc, dt_raw, conv_w, conv_b, dt_bias, a_log, d_skip, norm_w):
    b, s, _ = z.shape
    xbc = jax.nn.silu(_causal_conv(xbc, conv_w, conv_b)).astype(jnp.float32)
    xs, bm, cm = jnp.split(xbc, [SSD_WIDTH, SSD_WIDTH + SSD_GROUPS * SSD_STATE], axis=-1)
    xs = xs.reshape(b, s, N_SSD_HEADS, SSD_HEAD_DIM)
    rep = N_SSD_HEADS // SSD_GROUPS
    bm = jnp.repeat(bm.reshape(b, s, SSD_GROUPS, SSD_STATE), rep, axis=2)
    cm = jnp.repeat(cm.reshape(b, s, SSD_GROUPS, SSD_STATE), rep, axis=2)
    dt = jax.nn.softplus(dt_raw.astype(jnp.float32) + dt_bias.astype(jnp.float32))
    a = -jnp.exp(a_log.astype(jnp.float32))
    y = _ssd_chunked(xs * dt[..., None], a * dt, bm, cm) + d_skip.astype(jnp.float32)[:, None] * xs
    y = y.reshape(b, s, SSD_WIDTH) * jax.nn.silu(z.astype(jnp.float32))
    y = y * lax.rsqrt(jnp.mean(jnp.square(y), axis=-1, keepdims=True) + RMS_EPS) * norm_w
    return y.astype(z.dtype)


def _memory_attention(qm, mem, w_mem_kv, mem_bias):
    b, s, _ = qm.shape
    km, vm = jnp.split(mem @ w_mem_kv, 2, axis=-1)
    km = km.reshape(b, N_MEM, N_MEM_HEADS, HEAD_DIM)
    vm = vm.reshape(b, N_MEM, N_MEM_HEADS, HEAD_DIM)
    q = qm.reshape(b, s, N_MEM_HEADS, HEAD_DIM) * HEAD_DIM ** -0.5
    sc = jnp.einsum('bshe,bmhe->bhsm', q, km).astype(jnp.float32) + mem_bias.astype(jnp.float32)[None, :, None, :]
    p = jax.nn.softmax(sc, axis=-1)
    o = jnp.einsum('bhsm,bmhe->bshe', p, vm.astype(jnp.float32))
    return o.reshape(b, s, MEM_WIDTH).astype(qm.dtype)


def _moe(x, w_router, b_router, w_gate, w_up, w_down):
    logits = (x @ w_router).astype(jnp.float32) + b_router.astype(jnp.float32)
    scores = jax.nn.softmax(logits, axis=-1)
    grp = scores.reshape(*scores.shape[:-1], N_EXPERT_GROUPS, EXPERTS_PER_GROUP)
    grp_score = lax.top_k(grp, TOP_K)[0].sum(-1)
    best = jnp.argmax(grp_score, axis=-1)
    in_grp = (jnp.arange(N_EXPERTS) // EXPERTS_PER_GROUP) == best[..., None]
    top_w, top_i = lax.top_k(jnp.where(in_grp, scores, -jnp.inf), TOP_K)
    top_w = top_w / top_w.sum(-1, keepdims=True)
    gates = (jax.nn.one_hot(top_i, N_EXPERTS, dtype=jnp.float32) * top_w[..., None]).sum(-2)
    out = jnp.zeros_like(x)
    for e in range(N_EXPERTS):
        h = jax.nn.silu(x @ w_gate[e]) * (x @ w_up[e])
        out = out + gates[..., e:e + 1].astype(x.dtype) * (h @ w_down[e])
    return out


def setup_inputs(seed: int = 0) -> dict:
    key = jax.random.key(seed)
    ks = jax.random.split(key, 22)
    f32 = jnp.float32

    def nrm(k, shape, scale):
        return jax.random.normal(k, shape, f32) * scale

    x = nrm(ks[0], (BATCH, SEQ, D_MODEL), 1.0)
    mem = nrm(ks[1], (BATCH, N_MEM, D_MODEL), 1.0)
    col_scale = np.ones((N_IN_COLS,), np.float32)
    col_scale[2 * ATTN_WIDTH:3 * ATTN_WIDTH] = BETA
    w_in = nrm(ks[2], (DEPTH, D_MODEL, N_IN_COLS), D_MODEL ** -0.5) * jnp.asarray(col_scale)
    w_out = nrm(ks[3], (DEPTH, MIX_WIDTH, D_MODEL), BETA * MIX_WIDTH ** -0.5)
    rel_bias = nrm(ks[4], (N_BUCKETS, N_ATTN_HEADS), 0.1)
    conv_w = nrm(ks[5], (DEPTH, CONV_WIDTH, CONV_CH), CONV_WIDTH ** -0.5)
    conv_b = nrm(ks[6], (DEPTH, CONV_CH), 0.02)
    dt = jnp.exp(jax.random.uniform(ks[7], (DEPTH, N_SSD_HEADS), f32, math.log(1e-3), math.log(1e-1)))
    dt_bias = dt + jnp.log(-jnp.expm1(-dt))
    a_log = jnp.log(jax.random.uniform(ks[8], (DEPTH, N_SSD_HEADS), f32, 1.0, 16.0))
    d_skip = 1.0 + nrm(ks[9], (DEPTH, N_SSD_HEADS), 0.1)
    ssd_norm_w = 1.0 + nrm(ks[10], (DEPTH, SSD_WIDTH), 0.02)
    kv_scale = jnp.concatenate([jnp.ones((MEM_WIDTH,), f32), jnp.full((MEM_WIDTH,), BETA, f32)])
    w_mem_kv = nrm(ks[11], (DEPTH, D_MODEL, 2 * MEM_WIDTH), D_MODEL ** -0.5) * kv_scale
    mem_bias = nrm(ks[12], (DEPTH, N_MEM_HEADS, N_MEM), 0.1)
    ln1_g = 1.0 + nrm(ks[13], (DEPTH, D_MODEL), 0.02)
    ln1_b = nrm(ks[14], (DEPTH, D_MODEL), 0.02)
    ln2_g = 1.0 + nrm(ks[15], (DEPTH, D_MODEL), 0.02)
    ln2_b = nrm(ks[16], (DEPTH, D_MODEL), 0.02)
    w_router = nrm(ks[17], (D_MODEL, N_EXPERTS), D_MODEL ** -0.5)
    b_router = nrm(ks[18], (N_EXPERTS,), 0.01)
    w_gate = nrm(ks[19], (DEPTH, N_EXPERTS, D_MODEL, D_EXPERT), D_MODEL ** -0.5)
    w_up = nrm(ks[20], (DEPTH, N_EXPERTS, D_MODEL, D_EXPERT), BETA * D_MODEL ** -0.5)
    w_down = nrm(ks[21], (DEPTH, N_EXPERTS, D_EXPERT, D_MODEL), BETA * D_EXPERT ** -0.5)
    return {'x': x, 'mem': mem, 'w_in': w_in, 'w_out': w_out, 'rel_bias': rel_bias,
            'conv_w': conv_w, 'conv_b': conv_b, 'dt_bias': dt_bias, 'a_log': a_log,
            'd_skip': d_skip, 'ssd_norm_w': ssd_norm_w, 'w_mem_kv': w_mem_kv, 'mem_bias': mem_bias,
            'ln1_g': ln1_g, 'ln1_b': ln1_b, 'ln2_g': ln2_g, 'ln2_b': ln2_b,
            'w_router': w_router, 'b_router': b_router, 'w_gate': w_gate, 'w_up': w_up, 'w_down': w_down}


def reference(x, mem, w_in, w_out, rel_bias, conv_w, conv_b, dt_bias, a_log, d_skip, ssd_norm_w,
              w_mem_kv, mem_bias, ln1_g, ln1_b, ln2_g, ln2_b, w_router, b_router, w_gate, w_up, w_down):
    b, s, _ = x.shape
    splits = np.cumsum([ATTN_WIDTH, ATTN_WIDTH, ATTN_WIDTH, SSD_WIDTH, CONV_CH, N_SSD_HEADS]).tolist()
    for l in range(DEPTH):
        proj = x @ w_in[l]
        q, k, v, z, xbc, dt_raw, qm = jnp.split(proj, splits, axis=-1)
        heads = lambda t: t.reshape(b, s, N_ATTN_HEADS, HEAD_DIM)
        attn = _dilated_attention(heads(q) * HEAD_DIM ** -0.5, heads(k), heads(v), rel_bias)
        attn = attn.reshape(b, s, ATTN_WIDTH).astype(x.dtype)
        ssd = _ssd_mixer(z, xbc, dt_raw, conv_w[l], conv_b[l], dt_bias[l], a_log[l], d_skip[l], ssd_norm_w[l])
        memo = _memory_attention(qm, mem, w_mem_kv[l], mem_bias[l])
        y = jnp.concatenate([attn, ssd, memo], axis=-1) @ w_out[l]
        x = _layer_norm(ALPHA * x + y, ln1_g[l], ln1_b[l])
        y = _moe(x, w_router, b_router, w_gate[l], w_up[l], w_down[l])
        x = _layer_norm(ALPHA * x + y, ln2_g[l], ln2_b[l])
    return x
```

```python
import functools
import math

import numpy as np
import jax
import jax.numpy as jnp
from jax import lax
from jax.experimental import pallas as pl
from jax.experimental.pallas import tpu as pltpu

F32 = jnp.float32
BF16 = jnp.bfloat16

D_MODEL = 1024
HEAD_DIM = 64
N_ATTN_HEADS = 8
ATTN_WIDTH = N_ATTN_HEADS * HEAD_DIM
DILATED_BRANCHES = ((128, 1), (512, 4), (2048, 16))
ATTN_BLK = 128
N_SSD_HEADS = 4
SSD_HEAD_DIM = 64
SSD_WIDTH = N_SSD_HEADS * SSD_HEAD_DIM
SSD_GROUPS = 2
SSD_STATE = 128
CONV_WIDTH = 4
SSD_CHUNK = 128
CONV_CH = SSD_WIDTH + 2 * SSD_GROUPS * SSD_STATE
N_MEM_HEADS = 4
MEM_WIDTH = N_MEM_HEADS * HEAD_DIM
N_MEM = 256
N_BUCKETS = 32
MAX_DISTANCE = 2048
N_EXPERTS = 16
N_EXPERT_GROUPS = 4
EXPERTS_PER_GROUP = 4
D_EXPERT = 1024
DEPTH = 2
ALPHA = (2 * DEPTH) ** 0.25
LN_EPS = 1e-5
RMS_EPS = 1e-5

LANES = 128
NEG = -1e30
PAIRS = ((0, 1), (0, 2), (0, 3), (1, 2), (1, 3), (2, 3))
N_CLASSES = N_EXPERT_GROUPS * len(PAIRS)
CLS_PAD = 32

_C_Q, _C_K, _C_V = 0, ATTN_WIDTH, 2 * ATTN_WIDTH
_C_Z = 3 * ATTN_WIDTH
_C_XBC = _C_Z + SSD_WIDTH
_C_QM = _C_XBC + CONV_CH
_C_DT = _C_QM + MEM_WIDTH
IN_COLS = _C_DT + LANES

IN_TM = 512
MEM_TQ = 512
SSD_TM = 512
OUT_TM = 512
MOE_TM = 256
VMEM_LIMIT = 56 * 1024 * 1024


def _cparams(sem):
    return pltpu.CompilerParams(dimension_semantics=sem, vmem_limit_bytes=VMEM_LIMIT)


def _dot(a, b):
    return jnp.dot(a, b, preferred_element_type=F32)


def _dot_nt(a, b):
    return lax.dot_general(a, b, (((1,), (1,)), ((), ())), preferred_element_type=F32)


def _inproj_kernel(x_ref, w_ref, q_ref, k_ref, v_ref, z_ref, xbc_ref, qm_ref, dt_ref):
    xb = x_ref[...].astype(BF16)
    for ref, lo, hi in ((q_ref, _C_Q, _C_K), (k_ref, _C_K, _C_V), (v_ref, _C_V, _C_Z), (z_ref, _C_Z, _C_XBC),
                        (xbc_ref, _C_XBC, _C_QM), (qm_ref, _C_QM, _C_DT), (dt_ref, _C_DT, IN_COLS)):
        ref[...] = _dot(xb, w_ref[:, lo:hi]).astype(ref.dtype)


def _inproj(x, w):
    t = x.shape[0]
    widths = (ATTN_WIDTH, ATTN_WIDTH, ATTN_WIDTH, SSD_WIDTH, CONV_CH, MEM_WIDTH, LANES)
    return pl.pallas_call(
        _inproj_kernel,
        grid=(t // IN_TM,),
        in_specs=[pl.BlockSpec((IN_TM, D_MODEL), lambda i: (i, 0)),
                  pl.BlockSpec((D_MODEL, IN_COLS), lambda i: (0, 0))],
        out_specs=[pl.BlockSpec((IN_TM, wd), lambda i: (i, 0)) for wd in widths],
        out_shape=[jax.ShapeDtypeStruct((t, wd), F32) for wd in widths],
        compiler_params=_cparams(("parallel",)),
        name="inproj",
    )(x, w)


def _attn_kernel(q_ref, k_ref, v_ref, bias_ref, o_ref, obuf, lbuf, *, seq):
    head0 = lax.broadcasted_iota(jnp.int32, (ATTN_BLK, LANES), 1) < HEAD_DIM
    for bi, (_, d) in enumerate(DILATED_BRANCHES):
        nb = seq // d // ATTN_BLK

        def block(idx, carry, bi=bi, d=d, nb=nb):
            r = idx // nb
            n = idx % nb
            qs = r + d * ATTN_BLK * n
            ks = r + d * ATTN_BLK * jnp.maximum(n - 1, 0)
            first = 1 - jnp.minimum(n, 1)
            qb = q_ref[0, pl.ds(qs, ATTN_BLK, stride=d), :]
            kb = k_ref[0, pl.ds(ks, 2 * ATTN_BLK, stride=d), :].astype(BF16)
            vb = v_ref[0, pl.ds(ks, 2 * ATTN_BLK, stride=d), :].astype(BF16)
            outs, lses = [], []
            for h in range(2):
                keep = head0 if h == 0 else jnp.logical_not(head0)
                qh = jnp.where(keep, qb, 0.0).astype(BF16)
                s = _dot_nt(qh, kb) + bias_ref[bi, first, h]
                m = jnp.max(s, axis=-1, keepdims=True)
                p = jnp.exp(s - m)
                den = jnp.sum(p, axis=-1, keepdims=True)
                outs.append(_dot(p.astype(BF16), vb) / den)
                lses.append(m + jnp.log(den))
            obuf[bi, pl.ds(qs, ATTN_BLK, stride=d), :] = jnp.where(head0, outs[0], outs[1])
            lbuf[bi, pl.ds(qs, ATTN_BLK, stride=d), :] = jnp.where(head0, lses[0], lses[1])
            return carry

        lax.fori_loop(0, d * nb, block, 0)

    rows = 256

    def merge(i, carry):
        sl = pl.ds(pl.multiple_of(i * rows, rows), rows)
        l0, l1, l2 = lbuf[0, sl, :], lbuf[1, sl, :], lbuf[2, sl, :]
        m = jnp.maximum(jnp.maximum(l0, l1), l2)
        w0, w1, w2 = jnp.exp(l0 - m), jnp.exp(l1 - m), jnp.exp(l2 - m)
        out = (w0 * obuf[0, sl, :] + w1 * obuf[1, sl, :] + w2 * obuf[2, sl, :]) / (w0 + w1 + w2)
        o_ref[0, sl, :] = out.astype(o_ref.dtype)
        return carry

    lax.fori_loop(0, seq // rows, merge, 0)


def _attention(q, k, v, bias, batch, seq):
    q3, k3, v3 = (a.reshape(batch, seq, ATTN_WIDTH) for a in (q, k, v))
    n_pairs = ATTN_WIDTH // LANES
    blk = pl.BlockSpec((1, seq, LANES), lambda b, hp: (b, 0, hp))
    out = pl.pallas_call(
        functools.partial(_attn_kernel, seq=seq),
        grid=(batch, n_pairs),
        in_specs=[blk, blk, blk,
                  pl.BlockSpec((3, 2, 2, ATTN_BLK, 2 * ATTN_BLK), lambda b, hp: (0, 0, hp, 0, 0))],
        out_specs=blk,
        out_shape=jax.ShapeDtypeStruct((batch, seq, ATTN_WIDTH), BF16),
        scratch_shapes=[pltpu.VMEM((3, seq, LANES), F32), pltpu.VMEM((3, seq, LANES), F32)],
        compiler_params=_cparams(("parallel", "parallel")),
        name="dilated_attn",
    )(q3, k3, v3, bias)
    return out.reshape(batch * seq, ATTN_WIDTH)


def _t5_bucket_np(dist):
    n = np.maximum(dist, 0)
    max_exact = N_BUCKETS // 2
    nf = np.maximum(n, max_exact).astype(np.float32)
    large = max_exact + (np.log(nf / np.float32(max_exact)) / np.float32(math.log(MAX_DISTANCE / max_exact))
                         * np.float32(N_BUCKETS - max_exact)).astype(np.int32)
    large = np.minimum(large, N_BUCKETS - 1)
    return np.where(n < max_exact, n, large)


def _attn_bias_tables(rel_bias):
    period = 3 * ATTN_BLK
    steps = np.arange(ATTN_BLK + 1)
    tables = []
    for _, d in DILATED_BRANCHES:
        onehot = np.eye(N_BUCKETS, dtype=np.float32)[_t5_bucket_np(steps * d)]
        tbl = jnp.dot(jnp.asarray(onehot), rel_bias.astype(F32), precision=lax.Precision.HIGHEST)
        masked = lambda n: jnp.full((n, N_ATTN_HEADS), NEG, F32)
        v_full = jnp.concatenate([tbl[::-1], masked(period - ATTN_BLK - 1)], axis=0)
        v_first = jnp.concatenate([tbl[:1], masked(period - ATTN_BLK), tbl[1:ATTN_BLK][::-1]], axis=0)
        variants = []
        for vec in (v_full, v_first):
            flat = jnp.tile(vec, (ATTN_BLK, 1))[:ATTN_BLK * (period - 1)]
            mat = flat.reshape(ATTN_BLK, period - 1, N_ATTN_HEADS)[:, :2 * ATTN_BLK]
            variants.append(mat.transpose(2, 0, 1))
        tables.append(jnp.stack(variants))
    return jnp.stack(tables)


def _per_head_lanes(cols, lane_idx):
    out = cols[:, N_SSD_HEADS - 1:N_SSD_HEADS]
    for h in range(N_SSD_HEADS - 2, -1, -1):
        out = jnp.where(lane_idx < (h + 1) * SSD_HEAD_DIM, cols[:, h:h + 1], out)
    return out


def _ssd_kernel(z_ref, xbc_ref, dt_ref, cw_ref, cb_ref, dtb_ref, a_ref, dsk_ref, nw_ref, o_ref,
                ext_ref, state_ref):
    j = pl.program_id(1)
    tm = xbc_ref.shape[0]
    halo = 8

    @pl.when(j == 0)
    def _():
        ext_ref[0:halo, :] = jnp.zeros((halo, CONV_CH), F32)
        state_ref[...] = jnp.zeros_like(state_ref)

    ext_ref[halo:halo + tm, :] = xbc_ref[...]

    c = SSD_CHUNK
    row_i = lax.broadcasted_iota(jnp.int32, (c, c), 0)
    col_i = lax.broadcasted_iota(jnp.int32, (c, c), 1)
    tril = row_i >= col_i
    tril_f = tril.astype(F32)
    lane256 = lax.broadcasted_iota(jnp.int32, (c, SSD_WIDTH), 1)
    head_even = lax.broadcasted_iota(jnp.int32, (c, LANES), 1) < SSD_HEAD_DIM
    a_row = -jnp.exp(a_ref[...])

    for ci in range(tm // c):
        base = halo + ci * c
        acc = cb_ref[...]
        for kk in range(CONV_WIDTH):
            shift = CONV_WIDTH - 1 - kk
            acc = acc + cw_ref[kk:kk + 1, :] * ext_ref[base - shift:base - shift + c, :]
        xc = acc * jax.nn.sigmoid(acc)
        xs = xc[:, :SSD_WIDTH]
        bm = xc[:, SSD_WIDTH:SSD_WIDTH + SSD_GROUPS * SSD_STATE]
        cm = xc[:, SSD_WIDTH + SSD_GROUPS * SSD_STATE:]

        dt = jax.nn.softplus(dt_ref[ci * c:(ci + 1) * c, :] + dtb_ref[...])
        adt = dt * a_row
        acs = jnp.dot(tril_f, adt, precision=lax.Precision.HIGHEST, preferred_element_type=F32)
        acs_t = acs.T
        a_last = acs[c - 1:c, :]

        dt_b = _per_head_lanes(dt, lane256)
        acs_b = _per_head_lanes(acs, lane256)
        alast_b = _per_head_lanes(a_last, lane256[:1])
        xdt = xs * dt_b
        xdt_bf = xdt.astype(BF16)
        xdec_bf = (xdt * jnp.exp(alast_b - acs_b)).astype(BF16)

        y_pairs, new_states = [], []
        for g in range(SSD_GROUPS):
            sl = slice(g * LANES, (g + 1) * LANES)
            bg = bm[:, sl]
            cg_bf = cm[:, sl].astype(BF16)
            gmat = _dot_nt(cg_bf, bg.astype(BF16))
            ys = []
            for hh in range(2):
                h = 2 * g + hh
                diff = acs[:, h:h + 1] - acs_t[h:h + 1, :]
                lmat = jnp.exp(jnp.where(tril, diff, NEG))
                ys.append(_dot((gmat * lmat).astype(BF16), xdt_bf[:, sl]))
            y_diag = jnp.where(head_even, ys[0], ys[1])
            y_off = _dot(cg_bf, state_ref[:, sl].astype(BF16))
            y_pairs.append((y_diag, y_off))
            new_states.append(_dot(bg.T.astype(BF16), xdec_bf[:, sl]))
        y_diag = jnp.concatenate([y_pairs[0][0], y_pairs[1][0]], axis=1)
        y_off = jnp.concatenate([y_pairs[0][1], y_pairs[1][1]], axis=1)
        state_ref[...] = state_ref[...] * jnp.exp(alast_b) + jnp.concatenate(new_states, axis=1)

        y = y_diag + y_off * jnp.exp(acs_b) + dsk_ref[...] * xs
        zc = z_ref[ci * c:(ci + 1) * c, :]
        y = y * (zc * jax.nn.sigmoid(zc))
        y = y * lax.rsqrt(jnp.mean(jnp.square(y), axis=-1, keepdims=True) + RMS_EPS) * nw_ref[...]
        o_ref[ci * c:(ci + 1) * c, :] = y.astype(o_ref.dtype)

    ext_ref[0:halo, :] = ext_ref[tm:tm + halo, :]


def _ssd(z, xbc, dt, conv_w, conv_b, dt_bias, a_log, d_skip, norm_w, batch, seq):
    t = z.shape[0]
    steps = seq // SSD_TM
    pad = lambda a: jnp.pad(a.astype(F32), (0, LANES - a.shape[0]))[None, :]
    row = lambda n: pl.BlockSpec((SSD_TM, n), lambda b, j: (b * steps + j, 0))
    const = lambda shape: pl.BlockSpec(shape, lambda b, j: (0, 0))
    return pl.pallas_call(
        _ssd_kernel,
        grid=(batch, steps),
        in_specs=[row(SSD_WIDTH), row(CONV_CH), row(LANES),
                  const((CONV_WIDTH, CONV_CH)), const((1, CONV_CH)), const((1, LANES)), const((1, LANES)),
                  const((1, SSD_WIDTH)), const((1, SSD_WIDTH))],
        out_specs=row(SSD_WIDTH),
        out_shape=jax.ShapeDtypeStruct((t, SSD_WIDTH), BF16),
        scratch_shapes=[pltpu.VMEM((SSD_TM + 8, CONV_CH), F32), pltpu.VMEM((SSD_STATE, SSD_WIDTH), F32)],
        compiler_params=_cparams(("parallel", "arbitrary")),
        name="ssd",
    )(z, xbc, dt, conv_w.astype(F32), conv_b.astype(F32)[None, :], pad(dt_bias), pad(a_log),
      jnp.repeat(d_skip.astype(F32), SSD_HEAD_DIM)[None, :], norm_w.astype(F32)[None, :])


def _mem_kernel(qm_ref, mem_ref, wkv_ref, mb_ref, o_ref, k_sc, v_sc):
    @pl.when(pl.program_id(1) == 0)
    def _():
        kv = _dot(mem_ref[0].astype(BF16), wkv_ref[...])
        k_sc[...] = kv[:, :MEM_WIDTH].astype(BF16)
        v_sc[...] = kv[:, MEM_WIDTH:].astype(BF16)

    tq = qm_ref.shape[0]
    head0 = lax.broadcasted_iota(jnp.int32, (tq, LANES), 1) < HEAD_DIM
    outs = []
    for hp in range(MEM_WIDTH // LANES):
        sl = slice(hp * LANES, (hp + 1) * LANES)
        qp = qm_ref[:, sl]
        kp = k_sc[:, sl]
        vp = v_sc[:, sl]
        res = []
        for hh in range(2):
            h = 2 * hp + hh
            keep = head0 if hh == 0 else jnp.logical_not(head0)
            qh = jnp.where(keep, qp, 0.0).astype(BF16)
            s = _dot_nt(qh, kp) + mb_ref[h:h + 1, :]
            m = jnp.max(s, axis=-1, keepdims=True)
            p = jnp.exp(s - m)
            den = jnp.sum(p, axis=-1, keepdims=True)
            res.append(_dot(p.astype(BF16), vp) / den)
        outs.append(jnp.where(head0, res[0], res[1]))
    o_ref[...] = jnp.concatenate(outs, axis=1).astype(o_ref.dtype)


def _memory_attention(qm, mem, wkv, mem_bias, batch, seq):
    t = qm.shape[0]
    steps = seq // MEM_TQ
    mb = jnp.pad(mem_bias.astype(F32), ((0, 8 - N_MEM_HEADS), (0, 0)))
    return pl.pallas_call(
        _mem_kernel,
        grid=(batch, steps),
        in_specs=[pl.BlockSpec((MEM_TQ, MEM_WIDTH), lambda b, j: (b * steps + j, 0)),
                  pl.BlockSpec((1, N_MEM, D_MODEL), lambda b, j: (b, 0, 0)),
                  pl.BlockSpec((D_MODEL, 2 * MEM_WIDTH), lambda b, j: (0, 0)),
                  pl.BlockSpec((8, N_MEM), lambda b, j: (0, 0))],
        out_specs=pl.BlockSpec((MEM_TQ, MEM_WIDTH), lambda b, j: (b * steps + j, 0)),
        out_shape=jax.ShapeDtypeStruct((t, MEM_WIDTH), BF16),
        scratch_shapes=[pltpu.VMEM((N_MEM, MEM_WIDTH), BF16), pltpu.VMEM((N_MEM, MEM_WIDTH), BF16)],
        compiler_params=_cparams(("parallel", "arbitrary")),
        name="mem_attn",
    )(qm, mem, wkv, mb)


def _layer_norm(h, g, b):
    mu = jnp.mean(h, axis=-1, keepdims=True)
    var = jnp.mean(jnp.square(h - mu), axis=-1, keepdims=True)
    return (h - mu) * lax.rsqrt(var + LN_EPS) * g + b


def _outproj_kernel(attn_ref, ssd_ref, memo_ref, x_ref, wo_ref, g_ref, b_ref, wr_ref, br_ref, tri_ref,
                    x1_ref, route_ref, cnt_ref, carry_ref):
    i = pl.program_id(0)

    @pl.when(i == 0)
    def _():
        carry_ref[...] = jnp.zeros_like(carry_ref)

    y = (_dot(attn_ref[...], wo_ref[0:ATTN_WIDTH, :])
         + _dot(ssd_ref[...], wo_ref[ATTN_WIDTH:ATTN_WIDTH + SSD_WIDTH, :])
         + _dot(memo_ref[...], wo_ref[ATTN_WIDTH + SSD_WIDTH:, :]))
    x1 = _layer_norm(ALPHA * x_ref[...] + y, g_ref[...], b_ref[...])
    x1_ref[...] = x1

    logits = lax.dot_general(wr_ref[...], x1, (((1,), (1,)), ((), ())), precision=lax.Precision.HIGHEST,
                             preferred_element_type=F32) + br_ref[...]
    mx = jnp.max(logits, axis=0, keepdims=True)
    ex = jnp.exp(logits - mx)
    sc = ex / jnp.sum(ex, axis=0, keepdims=True)
    s = [sc[e:e + 1, :] for e in range(N_EXPERTS)]

    def top2_sum(vals):
        best = None
        for a, b in PAIRS:
            pair = vals[a] + vals[b]
            best = pair if best is None else jnp.maximum(best, pair)
        return best

    grp = [top2_sum(s[4 * g:4 * g + 4]) for g in range(N_EXPERT_GROUPS)]
    best_g = jnp.zeros_like(grp[0], dtype=jnp.int32)
    best_v = grp[0]
    for g in range(1, N_EXPERT_GROUPS):
        upd = grp[g] > best_v
        best_g = jnp.where(upd, g, best_g)
        best_v = jnp.where(upd, grp[g], best_v)
    sel = []
    for jj in range(EXPERTS_PER_GROUP):
        v = s[3 * EXPERTS_PER_GROUP + jj]
        for g in range(N_EXPERT_GROUPS - 2, -1, -1):
            v = jnp.where(best_g == g, s[g * EXPERTS_PER_GROUP + jj], v)
        sel.append(v)

    def argmax_first(vals):
        idx = jnp.zeros_like(best_g)
        val = vals[0]
        for jj in range(1, len(vals)):
            upd = vals[jj] > val
            idx = jnp.where(upd, jj, idx)
            val = jnp.where(upd, vals[jj], val)
        return idx, val

    i1, v1 = argmax_first(sel)
    i2, v2 = argmax_first([jnp.where(i1 == jj, -1.0, sel[jj]) for jj in range(EXPERTS_PER_GROUP)])
    tot = v1 + v2
    w1, w2 = v1 / tot, v2 / tot
    lo = jnp.minimum(i1, i2)
    hi = jnp.maximum(i1, i2)
    g_lo = jnp.where(i1 < i2, w1, w2)
    g_hi = jnp.where(i1 < i2, w2, w1)
    pair_idx = jnp.where(lo == 0, hi - 1, jnp.where(lo == 1, hi + 1, 5))
    cls = best_g * len(PAIRS) + pair_idx

    tm = cls.shape[1]
    onehot = (lax.broadcasted_iota(jnp.int32, (CLS_PAD, tm), 0) == cls)
    prefix = _dot(onehot.astype(BF16), tri_ref[...])
    oh = onehot.astype(F32)
    carry = carry_ref[:, 0:1]
    rank = jnp.sum(oh * (prefix + carry), axis=0, keepdims=True)
    new_carry = carry + jnp.sum(oh, axis=1, keepdims=True)
    carry_ref[...] = jnp.broadcast_to(new_carry, carry_ref.shape)
    cnt_ref[...] = jnp.broadcast_to(new_carry, cnt_ref.shape)

    route_ref[...] = jnp.concatenate(
        [cls.astype(F32), rank, g_lo, g_hi, jnp.zeros((4, tm), F32)], axis=0)


def _outproj_route(attn, ssd, memo, x, wo, g, b, wr_t, br_col, tri):
    t = x.shape[0]
    row = lambda n: pl.BlockSpec((OUT_TM, n), lambda i: (i, 0))
    const = lambda shape: pl.BlockSpec(shape, lambda i: (0, 0))
    return pl.pallas_call(
        _outproj_kernel,
        grid=(t // OUT_TM,),
        in_specs=[row(ATTN_WIDTH), row(SSD_WIDTH), row(MEM_WIDTH), row(D_MODEL),
                  const((D_MODEL, D_MODEL)), const((1, D_MODEL)), const((1, D_MODEL)),
                  const((N_EXPERTS, D_MODEL)), const((N_EXPERTS, 1)), const((OUT_TM, OUT_TM))],
        out_specs=[row(D_MODEL), pl.BlockSpec((8, OUT_TM), lambda i: (0, i)), const((CLS_PAD, LANES))],
        out_shape=[jax.ShapeDtypeStruct((t, D_MODEL), F32), jax.ShapeDtypeStruct((8, t), F32),
                   jax.ShapeDtypeStruct((CLS_PAD, LANES), F32)],
        scratch_shapes=[pltpu.VMEM((CLS_PAD, LANES), F32)],
        compiler_params=_cparams(("arbitrary",)),
        name="outproj_route",
    )(attn, ssd, memo, x, wo, g, b, wr_t, br_col, tri)


def _moe_kernel(ea_ref, eb_ref, nv_ref, tok_ref,
                x1_hbm, gates_ref, wga, wua, wda, wgb, wub, wdb, g_ref, b_ref,
                out_hbm, xbuf, obuf, gsem, ssem):
    i = pl.program_id(0)
    nt = pl.num_programs(0)
    slot = i % 2

    def gather_copy(tok, r, sl):
        return pltpu.make_async_copy(x1_hbm.at[pl.ds(tok, 1), :], xbuf.at[sl, pl.ds(r, 1), :], gsem.at[sl])

    def scatter_copy(tok, r, sl):
        return pltpu.make_async_copy(obuf.at[sl, pl.ds(r, 1), :], out_hbm.at[pl.ds(tok, 1), :], ssem.at[sl])

    def start_rows(make, tile, sl):
        base = tile * MOE_TM

        def body(r, carry):
            make(tok_ref[base + r], r, sl).start()
            return carry

        lax.fori_loop(0, nv_ref[tile], body, 0)

    def wait_rows(n, make_wait):
        size = MOE_TM
        while size >= 8:
            @pl.when((n & size) != 0)
            def _(size=size):
                make_wait(size).wait()
            size //= 2

        def body(r, carry):
            make_wait(1).wait()
            return carry

        lax.fori_loop(0, n & 7, body, 0)

    def wait_gather(tile, sl):
        wait_rows(nv_ref[tile], lambda rows: pltpu.make_async_copy(
            x1_hbm.at[pl.ds(0, rows), :], xbuf.at[sl, pl.ds(0, rows), :], gsem.at[sl]))

    def wait_scatter(tile, sl):
        wait_rows(nv_ref[tile], lambda rows: pltpu.make_async_copy(
            obuf.at[sl, pl.ds(0, rows), :], out_hbm.at[pl.ds(0, rows), :], ssem.at[sl]))

    @pl.when(i == 0)
    def _():
        xbuf[...] = jnp.zeros_like(xbuf)
        start_rows(gather_copy, 0, 0)

    wait_gather(i, slot)

    @pl.when(i + 1 < nt)
    def _():
        start_rows(gather_copy, i + 1, 1 - slot)

    x = xbuf[slot]
    xb = x.astype(BF16)
    half = D_EXPERT // 2

    def ffn(wg, wu, wd):
        acc = None
        for c in range(2):
            cs = slice(c * half, (c + 1) * half)
            gate = _dot(xb, wg[0, :, cs])
            up = _dot(xb, wu[0, :, cs])
            hid = (gate * jax.nn.sigmoid(gate) * up).astype(BF16)
            part = _dot(hid, wd[0, cs, :])
            acc = part if acc is None else acc + part
        return acc

    y = gates_ref[:, 0:1] * ffn(wga, wua, wda) + gates_ref[:, 1:2] * ffn(wgb, wub, wdb)
    out = _layer_norm(ALPHA * x + y, g_ref[...], b_ref[...])

    @pl.when(i >= 2)
    def _():
        wait_scatter(i - 2, slot)

    obuf[slot] = out
    start_rows(scatter_copy, i, slot)

    @pl.when(i == nt - 1)
    def _():
        @pl.when(i >= 1)
        def _():
            wait_scatter(i - 1, 1 - slot)

        wait_scatter(i, slot)


def _moe(x1, tok_of_slot, gates_slot, tile_ea, tile_eb, tile_nv, wg, wu, wd, g, b):
    t = x1.shape[0]
    nt = tile_ea.shape[0]
    wspec = lambda which: pl.BlockSpec(
        (1, D_MODEL, D_EXPERT), lambda i, ea, eb, nv, tok: ((ea if which == 0 else eb)[i], 0, 0))
    const = pl.BlockSpec((1, D_MODEL), lambda i, ea, eb, nv, tok: (0, 0))
    grid_spec = pltpu.PrefetchScalarGridSpec(
        num_scalar_prefetch=4,
        grid=(nt,),
        in_specs=[pl.BlockSpec(memory_space=pl.ANY),
                  pl.BlockSpec((MOE_TM, 2), lambda i, ea, eb, nv, tok: (i, 0)),
                  wspec(0), wspec(0), wspec(0), wspec(1), wspec(1), wspec(1), const, const],
        out_specs=pl.BlockSpec(memory_space=pl.ANY),
        scratch_shapes=[pltpu.VMEM((2, MOE_TM, D_MODEL), F32), pltpu.VMEM((2, MOE_TM, D_MODEL), F32),
                        pltpu.SemaphoreType.DMA((2,)), pltpu.SemaphoreType.DMA((2,))],
    )
    return pl.pallas_call(
        _moe_kernel,
        grid_spec=grid_spec,
        out_shape=jax.ShapeDtypeStruct((t, D_MODEL), F32),
        compiler_params=_cparams(("arbitrary",)),
        name="moe_experts",
    )(tile_ea, tile_eb, tile_nv, tok_of_slot, x1, gates_slot, wg, wu, wd, wg, wu, wd, g, b)


def _routing_tables(route, counts, t):
    n_tiles = t // MOE_TM + N_CLASSES
    cls = route[0].astype(jnp.int32)
    rank = route[1].astype(jnp.int32)
    cnt = counts[:N_CLASSES, 0].astype(jnp.int32)
    tiles_per_cls = (cnt + MOE_TM - 1) // MOE_TM
    tile_end = jnp.cumsum(tiles_per_cls)
    tile_start = tile_end - tiles_per_cls
    dest = tile_start[cls] * MOE_TM + rank
    tok_of_slot = jnp.zeros((n_tiles * MOE_TM,), jnp.int32).at[dest].set(jnp.arange(t, dtype=jnp.int32))
    gates_slot = jnp.zeros((n_tiles * MOE_TM, 2), F32).at[dest].set(jnp.stack([route[2], route[3]], axis=1))
    tile_id = jnp.arange(n_tiles, dtype=jnp.int32)
    used = tile_id < tile_end[-1]
    tile_cls = jnp.minimum(jnp.searchsorted(tile_end, jnp.minimum(tile_id, tile_end[-1] - 1), side="right"),
                           N_CLASSES - 1).astype(jnp.int32)
    tile_nv = jnp.where(used, jnp.clip(cnt[tile_cls] - (tile_id - tile_start[tile_cls]) * MOE_TM, 0, MOE_TM), 0)
    pair_lo = jnp.asarray([p[0] for p in PAIRS], jnp.int32)
    pair_hi = jnp.asarray([p[1] for p in PAIRS], jnp.int32)
    grp_base = (tile_cls // len(PAIRS)) * EXPERTS_PER_GROUP
    tile_ea = grp_base + pair_lo[tile_cls % len(PAIRS)]
    tile_eb = grp_base + pair_hi[tile_cls % len(PAIRS)]
    return tok_of_slot, gates_slot, tile_ea, tile_eb, tile_nv.astype(jnp.int32)


def _rearranged_w_in(w):
    scale = HEAD_DIM ** -0.5
    s = np.cumsum([ATTN_WIDTH, ATTN_WIDTH, ATTN_WIDTH, SSD_WIDTH, CONV_CH, N_SSD_HEADS]).tolist()
    q, k, v, z, xbc, dt, qm = (w[:, a:b] for a, b in zip([0] + s, s + [w.shape[1]]))
    dt = jnp.pad(dt, ((0, 0), (0, LANES - N_SSD_HEADS)))
    return jnp.concatenate([q * scale, k, v, z, xbc, qm * scale, dt], axis=1).astype(BF16)


def kernel(x, mem, w_in, w_out, rel_bias, conv_w, conv_b, dt_bias, a_log, d_skip, ssd_norm_w, w_mem_kv, mem_bias,
           ln1_g, ln1_b, ln2_g, ln2_b, w_router, b_router, w_gate, w_up, w_down):
    batch, seq, _ = x.shape
    t = batch * seq
    assert seq % (16 * ATTN_BLK) == 0 and seq % SSD_TM == 0 and t % MOE_TM == 0
    bias = _attn_bias_tables(rel_bias)
    tri = jnp.asarray(np.triu(np.ones((OUT_TM, OUT_TM), np.float32), k=1), BF16)
    wr_t = w_router.astype(F32).T
    br_col = b_router.astype(F32)[:, None]
    h = x.reshape(t, D_MODEL)
    for l in range(DEPTH):
        q, k, v, z, xbc, qm, dt = _inproj(h, _rearranged_w_in(w_in[l]))
        attn = _attention(q, k, v, bias, batch, seq)
        ssd = _ssd(z, xbc, dt, conv_w[l], conv_b[l], dt_bias[l], a_log[l], d_skip[l], ssd_norm_w[l], batch, seq)
        memo = _memory_attention(qm, mem, w_mem_kv[l].astype(BF16), mem_bias[l], batch, seq)
        x1, route, counts = _outproj_route(attn, ssd, memo, h, w_out[l].astype(BF16),
                                           ln1_g[l][None, :], ln1_b[l][None, :], wr_t, br_col, tri)
        tables = _routing_tables(route, counts, t)
        h = _moe(x1, *tables, w_gate[l].astype(BF16), w_up[l].astype(BF16), w_down[l].astype(BF16),
                 ln2_g[l][None, :], ln2_b[l][None, :])
    return h.reshape(batch, seq, D_MODEL)
```

```python
import functools
import math

import numpy as np
import jax
import jax.numpy as jnp
from jax import lax
from jax.experimental import pallas as pl
from jax.experimental.pallas import tpu as pltpu

F32 = jnp.float32
BF16 = jnp.bfloat16

D_MODEL = 1024
HEAD_DIM = 64
N_ATTN_HEADS = 8
ATTN_WIDTH = N_ATTN_HEADS * HEAD_DIM
DILATED_BRANCHES = ((128, 1), (512, 4), (2048, 16))
ATTN_BLK = 128
ATTN_UNROLL = 8
N_SSD_HEADS = 4
SSD_HEAD_DIM = 64
SSD_WIDTH = N_SSD_HEADS * SSD_HEAD_DIM
SSD_GROUPS = 2
SSD_STATE = 128
CONV_WIDTH = 4
SSD_CHUNK = 128
CONV_CH = SSD_WIDTH + 2 * SSD_GROUPS * SSD_STATE
N_MEM_HEADS = 4
MEM_WIDTH = N_MEM_HEADS * HEAD_DIM
N_MEM = 256
N_BUCKETS = 32
MAX_DISTANCE = 2048
N_EXPERTS = 16
N_EXPERT_GROUPS = 4
EXPERTS_PER_GROUP = 4
D_EXPERT = 1024
DEPTH = 2
ALPHA = (2 * DEPTH) ** 0.25
LN_EPS = 1e-5
RMS_EPS = 1e-5

LANES = 128
NEG = -1e30
PAIRS = ((0, 1), (0, 2), (0, 3), (1, 2), (1, 3), (2, 3))
N_CLASSES = N_EXPERT_GROUPS * len(PAIRS)
CLS_PAD = 32

_C_Q, _C_K, _C_V = 0, ATTN_WIDTH, 2 * ATTN_WIDTH
_C_Z = 3 * ATTN_WIDTH
_C_XBC = _C_Z + SSD_WIDTH
_C_QM = _C_XBC + CONV_CH
_C_DT = _C_QM + MEM_WIDTH
IN_COLS = _C_DT + LANES

IN_TM = 512
MEM_TQ = 512
SSD_TM = 512
OUT_TM = 512
MOE_TM = 256
VMEM_LIMIT = 56 * 1024 * 1024


def _cparams(sem):
    return pltpu.CompilerParams(dimension_semantics=sem, vmem_limit_bytes=VMEM_LIMIT)


def _dot(a, b):
    return jnp.dot(a, b, preferred_element_type=F32)


def _dot_nt(a, b):
    return lax.dot_general(a, b, (((1,), (1,)), ((), ())), preferred_element_type=F32)


def _inproj_kernel(x_ref, w_ref, q_ref, k_ref, v_ref, z_ref, xbc_ref, qm_ref, dt_ref):
    xb = x_ref[...].astype(BF16)
    for ref, lo, hi in ((q_ref, _C_Q, _C_K), (k_ref, _C_K, _C_V), (v_ref, _C_V, _C_Z), (z_ref, _C_Z, _C_XBC),
                        (xbc_ref, _C_XBC, _C_QM), (qm_ref, _C_QM, _C_DT), (dt_ref, _C_DT, IN_COLS)):
        ref[...] = _dot(xb, w_ref[:, lo:hi]).astype(ref.dtype)


def _inproj(x, w):
    t = x.shape[0]
    widths = (ATTN_WIDTH, ATTN_WIDTH, ATTN_WIDTH, SSD_WIDTH, CONV_CH, MEM_WIDTH, LANES)
    return pl.pallas_call(
        _inproj_kernel,
        grid=(t // IN_TM,),
        in_specs=[pl.BlockSpec((IN_TM, D_MODEL), lambda i: (i, 0)),
                  pl.BlockSpec((D_MODEL, IN_COLS), lambda i: (0, 0))],
        out_specs=[pl.BlockSpec((IN_TM, wd), lambda i: (i, 0)) for wd in widths],
        out_shape=[jax.ShapeDtypeStruct((t, wd), F32) for wd in widths],
        compiler_params=_cparams(("parallel",)),
        name="inproj",
    )(x, w)


def _attn_kernel(q_ref, k_ref, v_ref, bias_ref, o_ref, obuf, lbuf, *, seq):
    head0 = lax.broadcasted_iota(jnp.int32, (ATTN_BLK, LANES), 1) < HEAD_DIM
    for bi, (_, d) in enumerate(DILATED_BRANCHES):
        nb = seq // d // ATTN_BLK

        def block(idx, carry, bi=bi, d=d, nb=nb):
            r = idx // nb
            n = idx % nb
            qs = r + d * ATTN_BLK * n
            ks = r + d * ATTN_BLK * jnp.maximum(n - 1, 0)
            first = 1 - jnp.minimum(n, 1)
            qb = q_ref[0, pl.ds(qs, ATTN_BLK, stride=d), :]
            kb = k_ref[0, pl.ds(ks, 2 * ATTN_BLK, stride=d), :].astype(BF16)
            vb = v_ref[0, pl.ds(ks, 2 * ATTN_BLK, stride=d), :].astype(BF16)
            outs, lses = [], []
            for h in range(2):
                keep = head0 if h == 0 else jnp.logical_not(head0)
                qh = jnp.where(keep, qb, 0.0).astype(BF16)
                s = _dot_nt(qh, kb) + bias_ref[bi, first, h]
                m = jnp.max(s, axis=-1, keepdims=True)
                p = jnp.exp(s - m)
                den = jnp.sum(p, axis=-1, keepdims=True)
                outs.append(_dot(p.astype(BF16), vb) / den)
                lses.append(m + jnp.log(den))
            obuf[bi, pl.ds(qs, ATTN_BLK, stride=d), :] = jnp.where(head0, outs[0], outs[1])
            lbuf[bi, pl.ds(qs, ATTN_BLK, stride=d), :] = jnp.where(head0, lses[0], lses[1])
            return carry

        lax.fori_loop(0, d * nb, block, 0, unroll=ATTN_UNROLL)

    rows = 256

    def merge(i, carry):
        sl = pl.ds(pl.multiple_of(i * rows, rows), rows)
        l0, l1, l2 = lbuf[0, sl, :], lbuf[1, sl, :], lbuf[2, sl, :]
        m = jnp.maximum(jnp.maximum(l0, l1), l2)
        w0, w1, w2 = jnp.exp(l0 - m), jnp.exp(l1 - m), jnp.exp(l2 - m)
        out = (w0 * obuf[0, sl, :] + w1 * obuf[1, sl, :] + w2 * obuf[2, sl, :]) / (w0 + w1 + w2)
        o_ref[0, sl, :] = out.astype(o_ref.dtype)
        return carry

    lax.fori_loop(0, seq // rows, merge, 0)


def _attention(q, k, v, bias, batch, seq):
    q3, k3, v3 = (a.reshape(batch, seq, ATTN_WIDTH) for a in (q, k, v))
    n_pairs = ATTN_WIDTH // LANES
    blk = pl.BlockSpec((1, seq, LANES), lambda b, hp: (b, 0, hp))
    out = pl.pallas_call(
        functools.partial(_attn_kernel, seq=seq),
        grid=(batch, n_pairs),
        in_specs=[blk, blk, blk,
                  pl.BlockSpec((3, 2, 2, ATTN_BLK, 2 * ATTN_BLK), lambda b, hp: (0, 0, hp, 0, 0))],
        out_specs=blk,
        out_shape=jax.ShapeDtypeStruct((batch, seq, ATTN_WIDTH), BF16),
        scratch_shapes=[pltpu.VMEM((3, seq, LANES), F32), pltpu.VMEM((3, seq, LANES), F32)],
        compiler_params=_cparams(("parallel", "parallel")),
        name="dilated_attn",
    )(q3, k3, v3, bias)
    return out.reshape(batch * seq, ATTN_WIDTH)


def _t5_bucket_np(dist):
    n = np.maximum(dist, 0)
    max_exact = N_BUCKETS // 2
    nf = np.maximum(n, max_exact).astype(np.float32)
    large = max_exact + (np.log(nf / np.float32(max_exact)) / np.float32(math.log(MAX_DISTANCE / max_exact))
                         * np.float32(N_BUCKETS - max_exact)).astype(np.int32)
    large = np.minimum(large, N_BUCKETS - 1)
    return np.where(n < max_exact, n, large)


def _attn_bias_tables(rel_bias):
    period = 3 * ATTN_BLK
    steps = np.arange(ATTN_BLK + 1)
    tables = []
    for _, d in DILATED_BRANCHES:
        onehot = np.eye(N_BUCKETS, dtype=np.float32)[_t5_bucket_np(steps * d)]
        tbl = jnp.dot(jnp.asarray(onehot), rel_bias.astype(F32), precision=lax.Precision.HIGHEST)
        masked = lambda n: jnp.full((n, N_ATTN_HEADS), NEG, F32)
        v_full = jnp.concatenate([tbl[::-1], masked(period - ATTN_BLK - 1)], axis=0)
        v_first = jnp.concatenate([tbl[:1], masked(period - ATTN_BLK), tbl[1:ATTN_BLK][::-1]], axis=0)
        variants = []
        for vec in (v_full, v_first):
            flat = jnp.tile(vec, (ATTN_BLK, 1))[:ATTN_BLK * (period - 1)]
            mat = flat.reshape(ATTN_BLK, period - 1, N_ATTN_HEADS)[:, :2 * ATTN_BLK]
            variants.append(mat.transpose(2, 0, 1))
        tables.append(jnp.stack(variants))
    return jnp.stack(tables)


def _per_head_lanes(cols, lane_idx):
    out = cols[:, N_SSD_HEADS - 1:N_SSD_HEADS]
    for h in range(N_SSD_HEADS - 2, -1, -1):
        out = jnp.where(lane_idx < (h + 1) * SSD_HEAD_DIM, cols[:, h:h + 1], out)
    return out


def _ssd_kernel(z_ref, xbc_ref, dt_ref, cw_ref, cb_ref, dtb_ref, a_ref, dsk_ref, nw_ref, o_ref,
                ext_ref, state_ref):
    j = pl.program_id(1)
    tm = xbc_ref.shape[0]
    halo = 8

    @pl.when(j == 0)
    def _():
        ext_ref[0:halo, :] = jnp.zeros((halo, CONV_CH), F32)
        state_ref[...] = jnp.zeros_like(state_ref)

    ext_ref[halo:halo + tm, :] = xbc_ref[...]

    c = SSD_CHUNK
    row_i = lax.broadcasted_iota(jnp.int32, (c, c), 0)
    col_i = lax.broadcasted_iota(jnp.int32, (c, c), 1)
    tril = row_i >= col_i
    tril_f = tril.astype(F32)
    lane256 = lax.broadcasted_iota(jnp.int32, (c, SSD_WIDTH), 1)
    head_even = lax.broadcasted_iota(jnp.int32, (c, LANES), 1) < SSD_HEAD_DIM
    a_row = -jnp.exp(a_ref[...])

    for ci in range(tm // c):
        base = halo + ci * c
        acc = cb_ref[...]
        for kk in range(CONV_WIDTH):
            shift = CONV_WIDTH - 1 - kk
            acc = acc + cw_ref[kk:kk + 1, :] * ext_ref[base - shift:base - shift + c, :]
        xc = acc * jax.nn.sigmoid(acc)
        xs = xc[:, :SSD_WIDTH]
        bm = xc[:, SSD_WIDTH:SSD_WIDTH + SSD_GROUPS * SSD_STATE]
        cm = xc[:, SSD_WIDTH + SSD_GROUPS * SSD_STATE:]

        dt = jax.nn.softplus(dt_ref[ci * c:(ci + 1) * c, :] + dtb_ref[...])
        adt = dt * a_row
        acs = jnp.dot(tril_f, adt, precision=lax.Precision.HIGHEST, preferred_element_type=F32)
        acs_t = acs.T
        a_last = acs[c - 1:c, :]

        dt_b = _per_head_lanes(dt, lane256)
        acs_b = _per_head_lanes(acs, lane256)
        alast_b = _per_head_lanes(a_last, lane256[:1])
        xdt = xs * dt_b
        xdt_bf = xdt.astype(BF16)
        xdec_bf = (xdt * jnp.exp(alast_b - acs_b)).astype(BF16)

        y_pairs, new_states = [], []
        for g in range(SSD_GROUPS):
            sl = slice(g * LANES, (g + 1) * LANES)
            bg = bm[:, sl]
            cg_bf = cm[:, sl].astype(BF16)
            gmat = _dot_nt(cg_bf, bg.astype(BF16))
            ys = []
            for hh in range(2):
                h = 2 * g + hh
                diff = acs[:, h:h + 1] - acs_t[h:h + 1, :]
                lmat = jnp.exp(jnp.where(tril, diff, NEG))
                ys.append(_dot((gmat * lmat).astype(BF16), xdt_bf[:, sl]))
            y_diag = jnp.where(head_even, ys[0], ys[1])
            y_off = _dot(cg_bf, state_ref[:, sl].astype(BF16))
            y_pairs.append((y_diag, y_off))
            new_states.append(_dot(bg.T.astype(BF16), xdec_bf[:, sl]))
        y_diag = jnp.concatenate([y_pairs[0][0], y_pairs[1][0]], axis=1)
        y_off = jnp.concatenate([y_pairs[0][1], y_pairs[1][1]], axis=1)
        state_ref[...] = state_ref[...] * jnp.exp(alast_b) + jnp.concatenate(new_states, axis=1)

        y = y_diag + y_off * jnp.exp(acs_b) + dsk_ref[...] * xs
        zc = z_ref[ci * c:(ci + 1) * c, :]
        y = y * (zc * jax.nn.sigmoid(zc))
        y = y * lax.rsqrt(jnp.mean(jnp.square(y), axis=-1, keepdims=True) + RMS_EPS) * nw_ref[...]
        o_ref[ci * c:(ci + 1) * c, :] = y.astype(o_ref.dtype)

    ext_ref[0:halo, :] = ext_ref[tm:tm + halo, :]


def _ssd(z, xbc, dt, conv_w, conv_b, dt_bias, a_log, d_skip, norm_w, batch, seq):
    t = z.shape[0]
    steps = seq // SSD_TM
    pad = lambda a: jnp.pad(a.astype(F32), (0, LANES - a.shape[0]))[None, :]
    row = lambda n: pl.BlockSpec((SSD_TM, n), lambda b, j: (b * steps + j, 0))
    const = lambda shape: pl.BlockSpec(shape, lambda b, j: (0, 0))
    return pl.pallas_call(
        _ssd_kernel,
        grid=(batch, steps),
        in_specs=[row(SSD_WIDTH), row(CONV_CH), row(LANES),
                  const((CONV_WIDTH, CONV_CH)), const((1, CONV_CH)), const((1, LANES)), const((1, LANES)),
                  const((1, SSD_WIDTH)), const((1, SSD_WIDTH))],
        out_specs=row(SSD_WIDTH),
        out_shape=jax.ShapeDtypeStruct((t, SSD_WIDTH), BF16),
        scratch_shapes=[pltpu.VMEM((SSD_TM + 8, CONV_CH), F32), pltpu.VMEM((SSD_STATE, SSD_WIDTH), F32)],
        compiler_params=_cparams(("parallel", "arbitrary")),
        name="ssd",
    )(z, xbc, dt, conv_w.astype(F32), conv_b.astype(F32)[None, :], pad(dt_bias), pad(a_log),
      jnp.repeat(d_skip.astype(F32), SSD_HEAD_DIM)[None, :], norm_w.astype(F32)[None, :])


def _mem_kernel(qm_ref, mem_ref, wkv_ref, mb_ref, o_ref, k_sc, v_sc):
    @pl.when(pl.program_id(1) == 0)
    def _():
        kv = _dot(mem_ref[0].astype(BF16), wkv_ref[...])
        k_sc[...] = kv[:, :MEM_WIDTH].astype(BF16)
        v_sc[...] = kv[:, MEM_WIDTH:].astype(BF16)

    tq = qm_ref.shape[0]
    head0 = lax.broadcasted_iota(jnp.int32, (tq, LANES), 1) < HEAD_DIM
    outs = []
    for hp in range(MEM_WIDTH // LANES):
        sl = slice(hp * LANES, (hp + 1) * LANES)
        qp = qm_ref[:, sl]
        kp = k_sc[:, sl]
        vp = v_sc[:, sl]
        res = []
        for hh in range(2):
            h = 2 * hp + hh
            keep = head0 if hh == 0 else jnp.logical_not(head0)
            qh = jnp.where(keep, qp, 0.0).astype(BF16)
            s = _dot_nt(qh, kp) + mb_ref[h:h + 1, :]
            m = jnp.max(s, axis=-1, keepdims=True)
            p = jnp.exp(s - m)
            den = jnp.sum(p, axis=-1, keepdims=True)
            res.append(_dot(p.astype(BF16), vp) / den)
        outs.append(jnp.where(head0, res[0], res[1]))
    o_ref[...] = jnp.concatenate(outs, axis=1).astype(o_ref.dtype)


def _memory_attention(qm, mem, wkv, mem_bias, batch, seq):
    t = qm.shape[0]
    steps = seq // MEM_TQ
    mb = jnp.pad(mem_bias.astype(F32), ((0, 8 - N_MEM_HEADS), (0, 0)))
    return pl.pallas_call(
        _mem_kernel,
        grid=(batch, steps),
        in_specs=[pl.BlockSpec((MEM_TQ, MEM_WIDTH), lambda b, j: (b * steps + j, 0)),
                  pl.BlockSpec((1, N_MEM, D_MODEL), lambda b, j: (b, 0, 0)),
                  pl.BlockSpec((D_MODEL, 2 * MEM_WIDTH), lambda b, j: (0, 0)),
                  pl.BlockSpec((8, N_MEM), lambda b, j: (0, 0))],
        out_specs=pl.BlockSpec((MEM_TQ, MEM_WIDTH), lambda b, j: (b * steps + j, 0)),
        out_shape=jax.ShapeDtypeStruct((t, MEM_WIDTH), BF16),
        scratch_shapes=[pltpu.VMEM((N_MEM, MEM_WIDTH), BF16), pltpu.VMEM((N_MEM, MEM_WIDTH), BF16)],
        compiler_params=_cparams(("parallel", "arbitrary")),
        name="mem_attn",
    )(qm, mem, wkv, mb)


def _layer_norm(h, g, b):
    mu = jnp.mean(h, axis=-1, keepdims=True)
    var = jnp.mean(jnp.square(h - mu), axis=-1, keepdims=True)
    return (h - mu) * lax.rsqrt(var + LN_EPS) * g + b


def _outproj_kernel(attn_ref, ssd_ref, memo_ref, x_ref, wo_ref, g_ref, b_ref, wr_ref, br_ref, tri_ref,
                    x1_ref, route_ref, cnt_ref, carry_ref):
    i = pl.program_id(0)

    @pl.when(i == 0)
    def _():
        carry_ref[...] = jnp.zeros_like(carry_ref)

    y = (_dot(attn_ref[...], wo_ref[0:ATTN_WIDTH, :])
         + _dot(ssd_ref[...], wo_ref[ATTN_WIDTH:ATTN_WIDTH + SSD_WIDTH, :])
         + _dot(memo_ref[...], wo_ref[ATTN_WIDTH + SSD_WIDTH:, :]))
    x1 = _layer_norm(ALPHA * x_ref[...] + y, g_ref[...], b_ref[...])
    x1_ref[...] = x1

    logits = lax.dot_general(wr_ref[...], x1, (((1,), (1,)), ((), ())), precision=lax.Precision.HIGHEST,
                             preferred_element_type=F32) + br_ref[...]
    mx = jnp.max(logits, axis=0, keepdims=True)
    ex = jnp.exp(logits - mx)
    sc = ex / jnp.sum(ex, axis=0, keepdims=True)
    s = [sc[e:e + 1, :] for e in range(N_EXPERTS)]

    def top2_sum(vals):
        best = None
        for a, b in PAIRS:
            pair = vals[a] + vals[b]
            best = pair if best is None else jnp.maximum(best, pair)
        return best

    grp = [top2_sum(s[4 * g:4 * g + 4]) for g in range(N_EXPERT_GROUPS)]
    best_g = jnp.zeros_like(grp[0], dtype=jnp.int32)
    best_v = grp[0]
    for g in range(1, N_EXPERT_GROUPS):
        upd = grp[g] > best_v
        best_g = jnp.where(upd, g, best_g)
        best_v = jnp.where(upd, grp[g], best_v)
    sel = []
    for jj in range(EXPERTS_PER_GROUP):
        v = s[3 * EXPERTS_PER_GROUP + jj]
        for g in range(N_EXPERT_GROUPS - 2, -1, -1):
            v = jnp.where(best_g == g, s[g * EXPERTS_PER_GROUP + jj], v)
        sel.append(v)

    def argmax_first(vals):
        idx = jnp.zeros_like(best_g)
        val = vals[0]
        for jj in range(1, len(vals)):
            upd = vals[jj] > val
            idx = jnp.where(upd, jj, idx)
            val = jnp.where(upd, vals[jj], val)
        return idx, val

    i1, v1 = argmax_first(sel)
    i2, v2 = argmax_first([jnp.where(i1 == jj, -1.0, sel[jj]) for jj in range(EXPERTS_PER_GROUP)])
    tot = v1 + v2
    w1, w2 = v1 / tot, v2 / tot
    lo = jnp.minimum(i1, i2)
    hi = jnp.maximum(i1, i2)
    g_lo = jnp.where(i1 < i2, w1, w2)
    g_hi = jnp.where(i1 < i2, w2, w1)
    pair_idx = jnp.where(lo == 0, hi - 1, jnp.where(lo == 1, hi + 1, 5))
    cls = best_g * len(PAIRS) + pair_idx

    tm = cls.shape[1]
    onehot = (lax.broadcasted_iota(jnp.int32, (CLS_PAD, tm), 0) == cls)
    prefix = _dot(onehot.astype(BF16), tri_ref[...])
    oh = onehot.astype(F32)
    carry = carry_ref[:, 0:1]
    rank = jnp.sum(oh * (prefix + carry), axis=0, keepdims=True)
    new_carry = carry + jnp.sum(oh, axis=1, keepdims=True)
    carry_ref[...] = jnp.broadcast_to(new_carry, carry_ref.shape)
    cnt_ref[...] = jnp.broadcast_to(new_carry, cnt_ref.shape)

    route_ref[...] = jnp.concatenate(
        [cls.astype(F32), rank, g_lo, g_hi, jnp.zeros((4, tm), F32)], axis=0)


def _outproj_route(attn, ssd, memo, x, wo, g, b, wr_t, br_col, tri):
    t = x.shape[0]
    row = lambda n: pl.BlockSpec((OUT_TM, n), lambda i: (i, 0))
    const = lambda shape: pl.BlockSpec(shape, lambda i: (0, 0))
    return pl.pallas_call(
        _outproj_kernel,
        grid=(t // OUT_TM,),
        in_specs=[row(ATTN_WIDTH), row(SSD_WIDTH), row(MEM_WIDTH), row(D_MODEL),
                  const((D_MODEL, D_MODEL)), const((1, D_MODEL)), const((1, D_MODEL)),
                  const((N_EXPERTS, D_MODEL)), const((N_EXPERTS, 1)), const((OUT_TM, OUT_TM))],
        out_specs=[row(D_MODEL), pl.BlockSpec((8, OUT_TM), lambda i: (0, i)), const((CLS_PAD, LANES))],
        out_shape=[jax.ShapeDtypeStruct((t, D_MODEL), F32), jax.ShapeDtypeStruct((8, t), F32),
                   jax.ShapeDtypeStruct((CLS_PAD, LANES), F32)],
        scratch_shapes=[pltpu.VMEM((CLS_PAD, LANES), F32)],
        compiler_params=_cparams(("arbitrary",)),
        name="outproj_route",
    )(attn, ssd, memo, x, wo, g, b, wr_t, br_col, tri)


def _moe_kernel(ea_ref, eb_ref, nv_ref, tok_ref,
                x1_hbm, gates_ref, wga, wua, wda, wgb, wub, wdb, g_ref, b_ref,
                out_hbm, xbuf, obuf, gsem, ssem):
    i = pl.program_id(0)
    nt = pl.num_programs(0)
    slot = i % 2

    def gather_copy(tok, r, sl):
        return pltpu.make_async_copy(x1_hbm.at[pl.ds(tok, 1), :], xbuf.at[sl, pl.ds(r, 1), :], gsem.at[sl])

    def scatter_copy(tok, r, sl):
        return pltpu.make_async_copy(obuf.at[sl, pl.ds(r, 1), :], out_hbm.at[pl.ds(tok, 1), :], ssem.at[sl])

    def start_rows(make, tile, sl):
        base = tile * MOE_TM

        def body(r, carry):
            make(tok_ref[base + r], r, sl).start()
            return carry

        lax.fori_loop(0, nv_ref[tile], body, 0)

    def wait_rows(n, make_wait):
        size = MOE_TM
        while size >= 8:
            @pl.when((n & size) != 0)
            def _(size=size):
                make_wait(size).wait()
            size //= 2

        def body(r, carry):
            make_wait(1).wait()
            return carry

        lax.fori_loop(0, n & 7, body, 0)

    def wait_gather(tile, sl):
        wait_rows(nv_ref[tile], lambda rows: pltpu.make_async_copy(
            x1_hbm.at[pl.ds(0, rows), :], xbuf.at[sl, pl.ds(0, rows), :], gsem.at[sl]))

    def wait_scatter(tile, sl):
        wait_rows(nv_ref[tile], lambda rows: pltpu.make_async_copy(
            obuf.at[sl, pl.ds(0, rows), :], out_hbm.at[pl.ds(0, rows), :], ssem.at[sl]))

    @pl.when(i == 0)
    def _():
        xbuf[...] = jnp.zeros_like(xbuf)
        start_rows(gather_copy, 0, 0)

    wait_gather(i, slot)

    @pl.when(i + 1 < nt)
    def _():
        start_rows(gather_copy, i + 1, 1 - slot)

    x = xbuf[slot]
    xb = x.astype(BF16)
    half = D_EXPERT // 2

    def ffn(wg, wu, wd):
        acc = None
        for c in range(2):
            cs = slice(c * half, (c + 1) * half)
            gate = _dot(xb, wg[0, :, cs])
            up = _dot(xb, wu[0, :, cs])
            hid = (gate * jax.nn.sigmoid(gate) * up).astype(BF16)
            part = _dot(hid, wd[0, cs, :])
            acc = part if acc is None else acc + part
        return acc

    y = gates_ref[:, 0:1] * ffn(wga, wua, wda) + gates_ref[:, 1:2] * ffn(wgb, wub, wdb)
    out = _layer_norm(ALPHA * x + y, g_ref[...], b_ref[...])

    @pl.when(i >= 2)
    def _():
        wait_scatter(i - 2, slot)

    obuf[slot] = out
    start_rows(scatter_copy, i, slot)

    @pl.when(i == nt - 1)
    def _():
        @pl.when(i >= 1)
        def _():
            wait_scatter(i - 1, 1 - slot)

        wait_scatter(i, slot)


def _moe(x1, tok_of_slot, gates_slot, tile_ea, tile_eb, tile_nv, wg, wu, wd, g, b):
    t = x1.shape[0]
    nt = tile_ea.shape[0]
    wspec = lambda which: pl.BlockSpec(
        (1, D_MODEL, D_EXPERT), lambda i, ea, eb, nv, tok: ((ea if which == 0 else eb)[i], 0, 0))
    const = pl.BlockSpec((1, D_MODEL), lambda i, ea, eb, nv, tok: (0, 0))
    grid_spec = pltpu.PrefetchScalarGridSpec(
        num_scalar_prefetch=4,
        grid=(nt,),
        in_specs=[pl.BlockSpec(memory_space=pl.ANY),
                  pl.BlockSpec((MOE_TM, 2), lambda i, ea, eb, nv, tok: (i, 0)),
                  wspec(0), wspec(0), wspec(0), wspec(1), wspec(1), wspec(1), const, const],
        out_specs=pl.BlockSpec(memory_space=pl.ANY),
        scratch_shapes=[pltpu.VMEM((2, MOE_TM, D_MODEL), F32), pltpu.VMEM((2, MOE_TM, D_MODEL), F32),
                        pltpu.SemaphoreType.DMA((2,)), pltpu.SemaphoreType.DMA((2,))],
    )
    return pl.pallas_call(
        _moe_kernel,
        grid_spec=grid_spec,
        out_shape=jax.ShapeDtypeStruct((t, D_MODEL), F32),
        compiler_params=_cparams(("arbitrary",)),
        name="moe_experts",
    )(tile_ea, tile_eb, tile_nv, tok_of_slot, x1, gates_slot, wg, wu, wd, wg, wu, wd, g, b)


def _routing_tables(route, counts, t):
    n_tiles = t // MOE_TM + N_CLASSES
    cls = route[0].astype(jnp.int32)
    rank = route[1].astype(jnp.int32)
    cnt = counts[:N_CLASSES, 0].astype(jnp.int32)
    tiles_per_cls = (cnt + MOE_TM - 1) // MOE_TM
    tile_end = jnp.cumsum(tiles_per_cls)
    tile_start = tile_end - tiles_per_cls
    dest = tile_start[cls] * MOE_TM + rank
    tok_of_slot = jnp.zeros((n_tiles * MOE_TM,), jnp.int32).at[dest].set(jnp.arange(t, dtype=jnp.int32))
    gates_slot = jnp.zeros((n_tiles * MOE_TM, 2), F32).at[dest].set(jnp.stack([route[2], route[3]], axis=1))
    tile_id = jnp.arange(n_tiles, dtype=jnp.int32)
    used = tile_id < tile_end[-1]
    tile_cls = jnp.minimum(jnp.searchsorted(tile_end, jnp.minimum(tile_id, tile_end[-1] - 1), side="right"),
                           N_CLASSES - 1).astype(jnp.int32)
    tile_nv = jnp.where(used, jnp.clip(cnt[tile_cls] - (tile_id - tile_start[tile_cls]) * MOE_TM, 0, MOE_TM), 0)
    pair_lo = jnp.asarray([p[0] for p in PAIRS], jnp.int32)
    pair_hi = jnp.asarray([p[1] for p in PAIRS], jnp.int32)
    grp_base = (tile_cls // len(PAIRS)) * EXPERTS_PER_GROUP
    tile_ea = grp_base + pair_lo[tile_cls % len(PAIRS)]
    tile_eb = grp_base + pair_hi[tile_cls % len(PAIRS)]
    return tok_of_slot, gates_slot, tile_ea, tile_eb, tile_nv.astype(jnp.int32)


def _cast_kernel(w_ref, o_ref):
    o_ref[0] = w_ref[0, 0].astype(o_ref.dtype)


def _expert_weights_bf16(w):
    depth, n_exp, kk, nn = w.shape
    return pl.pallas_call(
        _cast_kernel,
        grid=(depth * n_exp,),
        in_specs=[pl.BlockSpec((1, 1, kk, nn), lambda i: (i // n_exp, i % n_exp, 0, 0))],
        out_specs=pl.BlockSpec((1, kk, nn), lambda i: (i, 0, 0)),
        out_shape=jax.ShapeDtypeStruct((depth * n_exp, kk, nn), BF16),
        compiler_params=_cparams(("parallel",)),
        name="expert_weight_cast",
    )(w)


def _rearranged_w_in(w):
    scale = HEAD_DIM ** -0.5
    s = np.cumsum([ATTN_WIDTH, ATTN_WIDTH, ATTN_WIDTH, SSD_WIDTH, CONV_CH, N_SSD_HEADS]).tolist()
    q, k, v, z, xbc, dt, qm = (w[:, a:b] for a, b in zip([0] + s, s + [w.shape[1]]))
    dt = jnp.pad(dt, ((0, 0), (0, LANES - N_SSD_HEADS)))
    return jnp.concatenate([q * scale, k, v, z, xbc, qm * scale, dt], axis=1).astype(BF16)


def kernel(x, mem, w_in, w_out, rel_bias, conv_w, conv_b, dt_bias, a_log, d_skip, ssd_norm_w, w_mem_kv, mem_bias,
           ln1_g, ln1_b, ln2_g, ln2_b, w_router, b_router, w_gate, w_up, w_down):
    batch, seq, _ = x.shape
    t = batch * seq
    assert seq % (16 * ATTN_BLK) == 0 and seq % SSD_TM == 0 and t % MOE_TM == 0
    bias = _attn_bias_tables(rel_bias)
    tri = jnp.asarray(np.triu(np.ones((OUT_TM, OUT_TM), np.float32), k=1), BF16)
    wr_t = w_router.astype(F32).T
    br_col = b_router.astype(F32)[:, None]
    wg, wu, wd = (_expert_weights_bf16(w) for w in (w_gate, w_up, w_down))
    h = x.reshape(t, D_MODEL)
    for l in range(DEPTH):
        q, k, v, z, xbc, qm, dt = _inproj(h, _rearranged_w_in(w_in[l]))
        attn = _attention(q, k, v, bias, batch, seq)
        ssd = _ssd(z, xbc, dt, conv_w[l], conv_b[l], dt_bias[l], a_log[l], d_skip[l], ssd_norm_w[l], batch, seq)
        memo = _memory_attention(qm, mem, w_mem_kv[l].astype(BF16), mem_bias[l], batch, seq)
        x1, route, counts = _outproj_route(attn, ssd, memo, h, w_out[l].astype(BF16),
                                           ln1_g[l][None, :], ln1_b[l][None, :], wr_t, br_col, tri)
        tok_of_slot, gates_slot, tile_ea, tile_eb, tile_nv = _routing_tables(route, counts, t)
        h = _moe(x1, tok_of_slot, gates_slot, tile_ea + l * N_EXPERTS, tile_eb + l * N_EXPERTS, tile_nv,
                 wg, wu, wd, ln2_g[l][None, :], ln2_b[l][None, :])
    return h.reshape(batch, seq, D_MODEL)
```

```python
import functools
import math

import numpy as np
import jax
import jax.numpy as jnp
from jax import lax
from jax.experimental import pallas as pl
from jax.experimental.pallas import tpu as pltpu

F32 = jnp.float32
BF16 = jnp.bfloat16

D_MODEL = 1024
HEAD_DIM = 64
N_ATTN_HEADS = 8
ATTN_WIDTH = N_ATTN_HEADS * HEAD_DIM
DILATED_BRANCHES = ((128, 1), (512, 4), (2048, 16))
ATTN_BLK = 128
ATTN_UNROLL = 8
N_SSD_HEADS = 4
SSD_HEAD_DIM = 64
SSD_WIDTH = N_SSD_HEADS * SSD_HEAD_DIM
SSD_GROUPS = 2
SSD_STATE = 128
CONV_WIDTH = 4
SSD_CHUNK = 128
CONV_CH = SSD_WIDTH + 2 * SSD_GROUPS * SSD_STATE
N_MEM_HEADS = 4
MEM_WIDTH = N_MEM_HEADS * HEAD_DIM
N_MEM = 256
N_BUCKETS = 32
MAX_DISTANCE = 2048
N_EXPERTS = 16
N_EXPERT_GROUPS = 4
EXPERTS_PER_GROUP = 4
D_EXPERT = 1024
DEPTH = 2
ALPHA = (2 * DEPTH) ** 0.25
LN_EPS = 1e-5
RMS_EPS = 1e-5

LANES = 128
NEG = -1e30
PAIRS = ((0, 1), (0, 2), (0, 3), (1, 2), (1, 3), (2, 3))
N_CLASSES = N_EXPERT_GROUPS * len(PAIRS)
CLS_PAD = 32

_C_Q, _C_K, _C_V = 0, ATTN_WIDTH, 2 * ATTN_WIDTH
_C_Z = 3 * ATTN_WIDTH
_C_XBC = _C_Z + SSD_WIDTH
_C_QM = _C_XBC + CONV_CH
_C_DT = _C_QM + MEM_WIDTH
IN_COLS = _C_DT + LANES
X1_COLS = D_MODEL + LANES

IN_TM = 512
MEM_TQ = 512
SSD_TM = 512
OUT_TM = 512
MOE_TM = 256
VMEM_LIMIT = 56 * 1024 * 1024


def _cparams(sem):
    return pltpu.CompilerParams(dimension_semantics=sem, vmem_limit_bytes=VMEM_LIMIT)


def _dot(a, b):
    return jnp.dot(a, b, preferred_element_type=F32)


def _dot_nt(a, b):
    return lax.dot_general(a, b, (((1,), (1,)), ((), ())), preferred_element_type=F32)


def _inproj_kernel(x_ref, w_ref, q_ref, k_ref, v_ref, z_ref, xbc_ref, qm_ref, dt_ref):
    xb = x_ref[...].astype(BF16)
    for ref, lo, hi in ((q_ref, _C_Q, _C_K), (k_ref, _C_K, _C_V), (v_ref, _C_V, _C_Z), (z_ref, _C_Z, _C_XBC),
                        (xbc_ref, _C_XBC, _C_QM), (qm_ref, _C_QM, _C_DT), (dt_ref, _C_DT, IN_COLS)):
        ref[...] = _dot(xb, w_ref[:, lo:hi]).astype(ref.dtype)


def _inproj(x, w, t):
    widths = (ATTN_WIDTH, ATTN_WIDTH, ATTN_WIDTH, SSD_WIDTH, CONV_CH, MEM_WIDTH, LANES)
    return pl.pallas_call(
        _inproj_kernel,
        grid=(t // IN_TM,),
        in_specs=[pl.BlockSpec((IN_TM, D_MODEL), lambda i: (i, 0)),
                  pl.BlockSpec((D_MODEL, IN_COLS), lambda i: (0, 0))],
        out_specs=[pl.BlockSpec((IN_TM, wd), lambda i: (i, 0)) for wd in widths],
        out_shape=[jax.ShapeDtypeStruct((t, wd), F32) for wd in widths],
        compiler_params=_cparams(("parallel",)),
        name="inproj",
    )(x, w)


def _attn_kernel(q_ref, k_ref, v_ref, bias_ref, o_ref, obuf, lbuf, *, seq):
    head0 = lax.broadcasted_iota(jnp.int32, (ATTN_BLK, LANES), 1) < HEAD_DIM
    for bi, (_, d) in enumerate(DILATED_BRANCHES):
        nb = seq // d // ATTN_BLK

        def block(idx, carry, bi=bi, d=d, nb=nb):
            r = idx // nb
            n = idx % nb
            qs = r + d * ATTN_BLK * n
            ks = r + d * ATTN_BLK * jnp.maximum(n - 1, 0)
            first = 1 - jnp.minimum(n, 1)
            qb = q_ref[0, pl.ds(qs, ATTN_BLK, stride=d), :]
            kb = k_ref[0, pl.ds(ks, 2 * ATTN_BLK, stride=d), :].astype(BF16)
            vb = v_ref[0, pl.ds(ks, 2 * ATTN_BLK, stride=d), :].astype(BF16)
            outs, lses = [], []
            for h in range(2):
                keep = head0 if h == 0 else jnp.logical_not(head0)
                qh = jnp.where(keep, qb, 0.0).astype(BF16)
                s = _dot_nt(qh, kb) + bias_ref[bi, first, h]
                m = jnp.max(s, axis=-1, keepdims=True)
                p = jnp.exp(s - m)
                den = jnp.sum(p, axis=-1, keepdims=True)
                outs.append(_dot(p.astype(BF16), vb) / den)
                lses.append(m + jnp.log(den))
            obuf[bi, pl.ds(qs, ATTN_BLK, stride=d), :] = jnp.where(head0, outs[0], outs[1])
            lbuf[bi, pl.ds(qs, ATTN_BLK, stride=d), :] = jnp.where(head0, lses[0], lses[1])
            return carry

        lax.fori_loop(0, d * nb, block, 0, unroll=ATTN_UNROLL)

    rows = 256

    def merge(i, carry):
        sl = pl.ds(pl.multiple_of(i * rows, rows), rows)
        l0, l1, l2 = lbuf[0, sl, :], lbuf[1, sl, :], lbuf[2, sl, :]
        m = jnp.maximum(jnp.maximum(l0, l1), l2)
        w0, w1, w2 = jnp.exp(l0 - m), jnp.exp(l1 - m), jnp.exp(l2 - m)
        out = (w0 * obuf[0, sl, :] + w1 * obuf[1, sl, :] + w2 * obuf[2, sl, :]) / (w0 + w1 + w2)
        o_ref[0, sl, :] = out.astype(o_ref.dtype)
        return carry

    lax.fori_loop(0, seq // rows, merge, 0)


def _attention(q, k, v, bias, batch, seq):
    q3, k3, v3 = (a.reshape(batch, seq, ATTN_WIDTH) for a in (q, k, v))
    n_pairs = ATTN_WIDTH // LANES
    blk = pl.BlockSpec((1, seq, LANES), lambda b, hp: (b, 0, hp))
    out = pl.pallas_call(
        functools.partial(_attn_kernel, seq=seq),
        grid=(batch, n_pairs),
        in_specs=[blk, blk, blk,
                  pl.BlockSpec((3, 2, 2, ATTN_BLK, 2 * ATTN_BLK), lambda b, hp: (0, 0, hp, 0, 0))],
        out_specs=blk,
        out_shape=jax.ShapeDtypeStruct((batch, seq, ATTN_WIDTH), BF16),
        scratch_shapes=[pltpu.VMEM((3, seq, LANES), F32), pltpu.VMEM((3, seq, LANES), F32)],
        compiler_params=_cparams(("parallel", "parallel")),
        name="dilated_attn",
    )(q3, k3, v3, bias)
    return out.reshape(batch * seq, ATTN_WIDTH)


def _t5_bucket_np(dist):
    n = np.maximum(dist, 0)
    max_exact = N_BUCKETS // 2
    nf = np.maximum(n, max_exact).astype(np.float32)
    large = max_exact + (np.log(nf / np.float32(max_exact)) / np.float32(math.log(MAX_DISTANCE / max_exact))
                         * np.float32(N_BUCKETS - max_exact)).astype(np.int32)
    large = np.minimum(large, N_BUCKETS - 1)
    return np.where(n < max_exact, n, large)


def _attn_bias_tables(rel_bias):
    period = 3 * ATTN_BLK
    steps = np.arange(ATTN_BLK + 1)
    tables = []
    for _, d in DILATED_BRANCHES:
        onehot = np.eye(N_BUCKETS, dtype=np.float32)[_t5_bucket_np(steps * d)]
        tbl = jnp.dot(jnp.asarray(onehot), rel_bias.astype(F32), precision=lax.Precision.HIGHEST)
        masked = lambda n: jnp.full((n, N_ATTN_HEADS), NEG, F32)
        v_full = jnp.concatenate([tbl[::-1], masked(period - ATTN_BLK - 1)], axis=0)
        v_first = jnp.concatenate([tbl[:1], masked(period - ATTN_BLK), tbl[1:ATTN_BLK][::-1]], axis=0)
        variants = []
        for vec in (v_full, v_first):
            flat = jnp.tile(vec, (ATTN_BLK, 1))[:ATTN_BLK * (period - 1)]
            mat = flat.reshape(ATTN_BLK, period - 1, N_ATTN_HEADS)[:, :2 * ATTN_BLK]
            variants.append(mat.transpose(2, 0, 1))
        tables.append(jnp.stack(variants))
    return jnp.stack(tables)


def _per_head_lanes(cols, lane_idx):
    out = cols[:, N_SSD_HEADS - 1:N_SSD_HEADS]
    for h in range(N_SSD_HEADS - 2, -1, -1):
        out = jnp.where(lane_idx < (h + 1) * SSD_HEAD_DIM, cols[:, h:h + 1], out)
    return out


def _ssd_kernel(z_ref, xbc_ref, dt_ref, cw_ref, cb_ref, dtb_ref, a_ref, dsk_ref, nw_ref, o_ref,
                ext_ref, state_ref):
    j = pl.program_id(1)
    tm = xbc_ref.shape[0]
    halo = 8

    @pl.when(j == 0)
    def _():
        ext_ref[0:halo, :] = jnp.zeros((halo, CONV_CH), F32)
        state_ref[...] = jnp.zeros_like(state_ref)

    ext_ref[halo:halo + tm, :] = xbc_ref[...]

    c = SSD_CHUNK
    row_i = lax.broadcasted_iota(jnp.int32, (c, c), 0)
    col_i = lax.broadcasted_iota(jnp.int32, (c, c), 1)
    tril = row_i >= col_i
    tril_f = tril.astype(F32)
    lane256 = lax.broadcasted_iota(jnp.int32, (c, SSD_WIDTH), 1)
    head_even = lax.broadcasted_iota(jnp.int32, (c, LANES), 1) < SSD_HEAD_DIM
    a_row = -jnp.exp(a_ref[...])

    for ci in range(tm // c):
        base = halo + ci * c
        acc = cb_ref[...]
        for kk in range(CONV_WIDTH):
            shift = CONV_WIDTH - 1 - kk
            acc = acc + cw_ref[kk:kk + 1, :] * ext_ref[base - shift:base - shift + c, :]
        xc = acc * jax.nn.sigmoid(acc)
        xs = xc[:, :SSD_WIDTH]
        bm = xc[:, SSD_WIDTH:SSD_WIDTH + SSD_GROUPS * SSD_STATE]
        cm = xc[:, SSD_WIDTH + SSD_GROUPS * SSD_STATE:]

        dt = jax.nn.softplus(dt_ref[ci * c:(ci + 1) * c, :] + dtb_ref[...])
        adt = dt * a_row
        acs = jnp.dot(tril_f, adt, precision=lax.Precision.HIGHEST, preferred_element_type=F32)
        acs_t = acs.T
        a_last = acs[c - 1:c, :]

        dt_b = _per_head_lanes(dt, lane256)
        acs_b = _per_head_lanes(acs, lane256)
        alast_b = _per_head_lanes(a_last, lane256[:1])
        xdt = xs * dt_b
        xdt_bf = xdt.astype(BF16)
        xdec_bf = (xdt * jnp.exp(alast_b - acs_b)).astype(BF16)

        y_pairs, new_states = [], []
        for g in range(SSD_GROUPS):
            sl = slice(g * LANES, (g + 1) * LANES)
            bg = bm[:, sl]
            cg_bf = cm[:, sl].astype(BF16)
            gmat = _dot_nt(cg_bf, bg.astype(BF16))
            ys = []
            for hh in range(2):
                h = 2 * g + hh
                diff = acs[:, h:h + 1] - acs_t[h:h + 1, :]
                lmat = jnp.exp(jnp.where(tril, diff, NEG))
                ys.append(_dot((gmat * lmat).astype(BF16), xdt_bf[:, sl]))
            y_diag = jnp.where(head_even, ys[0], ys[1])
            y_off = _dot(cg_bf, state_ref[:, sl].astype(BF16))
            y_pairs.append((y_diag, y_off))
            new_states.append(_dot(bg.T.astype(BF16), xdec_bf[:, sl]))
        y_diag = jnp.concatenate([y_pairs[0][0], y_pairs[1][0]], axis=1)
        y_off = jnp.concatenate([y_pairs[0][1], y_pairs[1][1]], axis=1)
        state_ref[...] = state_ref[...] * jnp.exp(alast_b) + jnp.concatenate(new_states, axis=1)

        y = y_diag + y_off * jnp.exp(acs_b) + dsk_ref[...] * xs
        zc = z_ref[ci * c:(ci + 1) * c, :]
        y = y * (zc * jax.nn.sigmoid(zc))
        y = y * lax.rsqrt(jnp.mean(jnp.square(y), axis=-1, keepdims=True) + RMS_EPS) * nw_ref[...]
        o_ref[ci * c:(ci + 1) * c, :] = y.astype(o_ref.dtype)

    ext_ref[0:halo, :] = ext_ref[tm:tm + halo, :]


def _ssd(z, xbc, dt, conv_w, conv_b, dt_bias, a_log, d_skip, norm_w, batch, seq):
    t = z.shape[0]
    steps = seq // SSD_TM
    pad = lambda a: jnp.pad(a.astype(F32), (0, LANES - a.shape[0]))[None, :]
    row = lambda n: pl.BlockSpec((SSD_TM, n), lambda b, j: (b * steps + j, 0))
    const = lambda shape: pl.BlockSpec(shape, lambda b, j: (0, 0))
    return pl.pallas_call(
        _ssd_kernel,
        grid=(batch, steps),
        in_specs=[row(SSD_WIDTH), row(CONV_CH), row(LANES),
                  const((CONV_WIDTH, CONV_CH)), const((1, CONV_CH)), const((1, LANES)), const((1, LANES)),
                  const((1, SSD_WIDTH)), const((1, SSD_WIDTH))],
        out_specs=row(SSD_WIDTH),
        out_shape=jax.ShapeDtypeStruct((t, SSD_WIDTH), BF16),
        scratch_shapes=[pltpu.VMEM((SSD_TM + 8, CONV_CH), F32), pltpu.VMEM((SSD_STATE, SSD_WIDTH), F32)],
        compiler_params=_cparams(("parallel", "arbitrary")),
        name="ssd",
    )(z, xbc, dt, conv_w.astype(F32), conv_b.astype(F32)[None, :], pad(dt_bias), pad(a_log),
      jnp.repeat(d_skip.astype(F32), SSD_HEAD_DIM)[None, :], norm_w.astype(F32)[None, :])


def _mem_kernel(qm_ref, mem_ref, wkv_ref, mb_ref, o_ref, k_sc, v_sc):
    @pl.when(pl.program_id(1) == 0)
    def _():
        kv = _dot(mem_ref[0].astype(BF16), wkv_ref[...])
        k_sc[...] = kv[:, :MEM_WIDTH].astype(BF16)
        v_sc[...] = kv[:, MEM_WIDTH:].astype(BF16)

    tq = qm_ref.shape[0]
    head0 = lax.broadcasted_iota(jnp.int32, (tq, LANES), 1) < HEAD_DIM
    outs = []
    for hp in range(MEM_WIDTH // LANES):
        sl = slice(hp * LANES, (hp + 1) * LANES)
        qp = qm_ref[:, sl]
        kp = k_sc[:, sl]
        vp = v_sc[:, sl]
        res = []
        for hh in range(2):
            h = 2 * hp + hh
            keep = head0 if hh == 0 else jnp.logical_not(head0)
            qh = jnp.where(keep, qp, 0.0).astype(BF16)
            s = _dot_nt(qh, kp) + mb_ref[h:h + 1, :]
            m = jnp.max(s, axis=-1, keepdims=True)
            p = jnp.exp(s - m)
            den = jnp.sum(p, axis=-1, keepdims=True)
            res.append(_dot(p.astype(BF16), vp) / den)
        outs.append(jnp.where(head0, res[0], res[1]))
    o_ref[...] = jnp.concatenate(outs, axis=1).astype(o_ref.dtype)


def _memory_attention(qm, mem, wkv, mem_bias, batch, seq):
    t = qm.shape[0]
    steps = seq // MEM_TQ
    mb = jnp.pad(mem_bias.astype(F32), ((0, 8 - N_MEM_HEADS), (0, 0)))
    return pl.pallas_call(
        _mem_kernel,
        grid=(batch, steps),
        in_specs=[pl.BlockSpec((MEM_TQ, MEM_WIDTH), lambda b, j: (b * steps + j, 0)),
                  pl.BlockSpec((1, N_MEM, D_MODEL), lambda b, j: (b, 0, 0)),
                  pl.BlockSpec((D_MODEL, 2 * MEM_WIDTH), lambda b, j: (0, 0)),
                  pl.BlockSpec((8, N_MEM), lambda b, j: (0, 0))],
        out_specs=pl.BlockSpec((MEM_TQ, MEM_WIDTH), lambda b, j: (b * steps + j, 0)),
        out_shape=jax.ShapeDtypeStruct((t, MEM_WIDTH), BF16),
        scratch_shapes=[pltpu.VMEM((N_MEM, MEM_WIDTH), BF16), pltpu.VMEM((N_MEM, MEM_WIDTH), BF16)],
        compiler_params=_cparams(("parallel", "arbitrary")),
        name="mem_attn",
    )(qm, mem, wkv, mb)


def _layer_norm(h, g, b):
    mu = jnp.mean(h, axis=-1, keepdims=True)
    var = jnp.mean(jnp.square(h - mu), axis=-1, keepdims=True)
    return (h - mu) * lax.rsqrt(var + LN_EPS) * g + b


def _outproj_kernel(attn_ref, ssd_ref, memo_ref, x_ref, wo_ref, g_ref, b_ref, wr_ref, br_ref, tri_ref,
                    x1_ref, route_ref, cnt_ref, carry_ref):
    i = pl.program_id(0)

    @pl.when(i == 0)
    def _():
        carry_ref[...] = jnp.zeros_like(carry_ref)

    y = (_dot(attn_ref[...], wo_ref[0:ATTN_WIDTH, :])
         + _dot(ssd_ref[...], wo_ref[ATTN_WIDTH:ATTN_WIDTH + SSD_WIDTH, :])
         + _dot(memo_ref[...], wo_ref[ATTN_WIDTH + SSD_WIDTH:, :]))
    x1 = _layer_norm(ALPHA * x_ref[...] + y, g_ref[...], b_ref[...])
    x1_ref[:, :D_MODEL] = x1

    logits = lax.dot_general(wr_ref[...], x1, (((1,), (1,)), ((), ())), precision=lax.Precision.HIGHEST,
                             preferred_element_type=F32) + br_ref[...]
    mx = jnp.max(logits, axis=0, keepdims=True)
    ex = jnp.exp(logits - mx)
    sc = ex / jnp.sum(ex, axis=0, keepdims=True)
    s = [sc[e:e + 1, :] for e in range(N_EXPERTS)]

    def top2_sum(vals):
        best = None
        for a, b in PAIRS:
            pair = vals[a] + vals[b]
            best = pair if best is None else jnp.maximum(best, pair)
        return best

    grp = [top2_sum(s[4 * g:4 * g + 4]) for g in range(N_EXPERT_GROUPS)]
    best_g = jnp.zeros_like(grp[0], dtype=jnp.int32)
    best_v = grp[0]
    for g in range(1, N_EXPERT_GROUPS):
        upd = grp[g] > best_v
        best_g = jnp.where(upd, g, best_g)
        best_v = jnp.where(upd, grp[g], best_v)
    sel = []
    for jj in range(EXPERTS_PER_GROUP):
        v = s[3 * EXPERTS_PER_GROUP + jj]
        for g in range(N_EXPERT_GROUPS - 2, -1, -1):
            v = jnp.where(best_g == g, s[g * EXPERTS_PER_GROUP + jj], v)
        sel.append(v)

    def argmax_first(vals):
        idx = jnp.zeros_like(best_g)
        val = vals[0]
        for jj in range(1, len(vals)):
            upd = vals[jj] > val
            idx = jnp.where(upd, jj, idx)
            val = jnp.where(upd, vals[jj], val)
        return idx, val

    i1, v1 = argmax_first(sel)
    i2, v2 = argmax_first([jnp.where(i1 == jj, -1.0, sel[jj]) for jj in range(EXPERTS_PER_GROUP)])
    tot = v1 + v2
    w1, w2 = v1 / tot, v2 / tot
    lo = jnp.minimum(i1, i2)
    hi = jnp.maximum(i1, i2)
    g_lo = jnp.where(i1 < i2, w1, w2)
    g_hi = jnp.where(i1 < i2, w2, w1)
    pair_idx = jnp.where(lo == 0, hi - 1, jnp.where(lo == 1, hi + 1, 5))
    cls = best_g * len(PAIRS) + pair_idx

    tm = cls.shape[1]
    onehot = (lax.broadcasted_iota(jnp.int32, (CLS_PAD, tm), 0) == cls)
    prefix = _dot(onehot.astype(BF16), tri_ref[...])
    oh = onehot.astype(F32)
    carry = carry_ref[:, 0:1]
    rank = jnp.sum(oh * (prefix + carry), axis=0, keepdims=True)
    new_carry = carry + jnp.sum(oh, axis=1, keepdims=True)
    carry_ref[...] = jnp.broadcast_to(new_carry, carry_ref.shape)
    cnt_ref[...] = jnp.broadcast_to(new_carry, cnt_ref.shape)

    route_ref[...] = jnp.concatenate(
        [cls.astype(F32), rank, g_lo, g_hi, jnp.zeros((4, tm), F32)], axis=0)
    gate_rows = jnp.concatenate([g_lo, g_hi, jnp.zeros((LANES - 2, tm), F32)], axis=0)
    x1_ref[:, D_MODEL:] = gate_rows.T


def _outproj_route(attn, ssd, memo, x, wo, g, b, wr_t, br_col, tri, t):
    row = lambda n: pl.BlockSpec((OUT_TM, n), lambda i: (i, 0))
    const = lambda shape: pl.BlockSpec(shape, lambda i: (0, 0))
    return pl.pallas_call(
        _outproj_kernel,
        grid=(t // OUT_TM,),
        in_specs=[row(ATTN_WIDTH), row(SSD_WIDTH), row(MEM_WIDTH), row(D_MODEL),
                  const((D_MODEL, D_MODEL)), const((1, D_MODEL)), const((1, D_MODEL)),
                  const((N_EXPERTS, D_MODEL)), const((N_EXPERTS, 1)), const((OUT_TM, OUT_TM))],
        out_specs=[row(X1_COLS), pl.BlockSpec((8, OUT_TM), lambda i: (0, i)), const((CLS_PAD, LANES))],
        out_shape=[jax.ShapeDtypeStruct((t, X1_COLS), F32), jax.ShapeDtypeStruct((8, t), F32),
                   jax.ShapeDtypeStruct((CLS_PAD, LANES), F32)],
        scratch_shapes=[pltpu.VMEM((CLS_PAD, LANES), F32)],
        compiler_params=_cparams(("arbitrary",)),
        name="outproj_route",
    )(attn, ssd, memo, x, wo, g, b, wr_t, br_col, tri)


def _moe_kernel(ea_ref, eb_ref, nv_ref, tok_ref,
                x1_hbm, wga, wua, wda, wgb, wub, wdb, g_ref, b_ref,
                out_hbm, xbuf0, xbuf1, obuf0, obuf1, gsem, ssem, *, n_tok):
    i = pl.program_id(0)
    n_used = nv_ref[pl.num_programs(0)]
    xbufs = (xbuf0, xbuf1)
    obufs = (obuf0, obuf1)

    def gather_copy(tok, r, p):
        src = x1_hbm.at[pl.ds(jnp.minimum(tok, n_tok - 1), 1), :]
        return pltpu.make_async_copy(src, xbufs[p].at[pl.ds(r, 1), :], gsem.at[p])

    def scatter_copy(dst, r, p):
        return pltpu.make_async_copy(obufs[p].at[pl.ds(r, 1), :], out_hbm.at[pl.ds(dst, 1), :], ssem.at[p])

    def tile_gather_done(p):
        return pltpu.make_async_copy(x1_hbm.at[pl.ds(0, MOE_TM), :], xbufs[p], gsem.at[p])

    def tile_scatter_done(p):
        return pltpu.make_async_copy(obufs[p], out_hbm.at[pl.ds(0, MOE_TM), :], ssem.at[p])

    @pl.when(i == 0)
    def _():
        for p in range(2):
            obufs[p][...] = jnp.zeros_like(obufs[p])
            spare = pltpu.make_async_copy(obufs[p], out_hbm.at[pl.ds(n_tok + p * MOE_TM, MOE_TM), :], ssem.at[p])
            spare.start()
            spare.wait()

        def first(r, carry):
            gather_copy(tok_ref[r], r, 0).start()
            return carry

        lax.fori_loop(0, MOE_TM, first, 0)

    def step(p):
        q = 1 - p
        tile_gather_done(p).wait()
        next_base = jnp.minimum(i + 1, n_used - 1) * MOE_TM
        prev_base = jnp.maximum(i - 1, 0) * MOE_TM
        is_first = i == 0

        def issue_rows(lo, hi):
            for r in range(lo, hi):
                gather_copy(tok_ref[next_base + r], r, q).start()
                dst = jnp.where(is_first, n_tok + q * MOE_TM + r, tok_ref[prev_base + r])
                scatter_copy(dst, r, q).start()

        x = xbufs[p][:, :D_MODEL]
        xb = x.astype(BF16)
        half = D_EXPERT // 2
        rows_per_chunk = MOE_TM // 4
        y = None
        for e, (wg, wu, wd) in enumerate(((wga, wua, wda), (wgb, wub, wdb))):
            acc = None
            for c in range(2):
                chunk = 2 * e + c
                issue_rows(chunk * rows_per_chunk, (chunk + 1) * rows_per_chunk)
                cs = slice(c * half, (c + 1) * half)
                gate = _dot(xb, wg[0, :, cs])
                up = _dot(xb, wu[0, :, cs])
                hid = (gate * jax.nn.sigmoid(gate) * up).astype(BF16)
                part = _dot(hid, wd[0, cs, :])
                acc = part if acc is None else acc + part
            term = xbufs[p][:, D_MODEL + e:D_MODEL + e + 1] * acc
            y = term if y is None else y + term
        out = _layer_norm(ALPHA * x + y, g_ref[...], b_ref[...])
        tile_scatter_done(q).wait()
        obufs[p][...] = out

        @pl.when(i == n_used - 1)
        def _():
            base = i * MOE_TM

            def last(r, carry):
                scatter_copy(tok_ref[base + r], r, p).start()
                return carry

            lax.fori_loop(0, MOE_TM, last, 0)
            tile_scatter_done(p).wait()
            tile_gather_done(q).wait()

    for p in range(2):
        @pl.when(jnp.logical_and(i < n_used, i % 2 == p))
        def _(p=p):
            step(p)


def _moe(x1, tok_of_slot, tile_ea, tile_eb, tile_nv, wg, wu, wd, g, b, n_tok):
    nt = tile_ea.shape[0]
    wspec = lambda which: pl.BlockSpec(
        (1, D_MODEL, D_EXPERT), lambda i, ea, eb, nv, tok: ((ea if which == 0 else eb)[i], 0, 0))
    const = pl.BlockSpec((1, D_MODEL), lambda i, ea, eb, nv, tok: (0, 0))
    grid_spec = pltpu.PrefetchScalarGridSpec(
        num_scalar_prefetch=4,
        grid=(nt,),
        in_specs=[pl.BlockSpec(memory_space=pl.ANY),
                  wspec(0), wspec(0), wspec(0), wspec(1), wspec(1), wspec(1), const, const],
        out_specs=pl.BlockSpec(memory_space=pl.ANY),
        scratch_shapes=[pltpu.VMEM((MOE_TM, X1_COLS), F32), pltpu.VMEM((MOE_TM, X1_COLS), F32),
                        pltpu.VMEM((MOE_TM, D_MODEL), F32), pltpu.VMEM((MOE_TM, D_MODEL), F32),
                        pltpu.SemaphoreType.DMA((2,)), pltpu.SemaphoreType.DMA((2,))],
    )
    return pl.pallas_call(
        functools.partial(_moe_kernel, n_tok=n_tok),
        grid_spec=grid_spec,
        out_shape=jax.ShapeDtypeStruct((n_tok + 2 * MOE_TM, D_MODEL), F32),
        compiler_params=_cparams(("arbitrary",)),
        name="moe_experts",
    )(tile_ea, tile_eb, tile_nv, tok_of_slot, x1, wg, wu, wd, wg, wu, wd, g, b)


def _routing_tables(route, counts, t):
    n_tiles = t // MOE_TM + N_CLASSES
    cls = route[0].astype(jnp.int32)
    rank = route[1].astype(jnp.int32)
    cnt = counts[:N_CLASSES, 0].astype(jnp.int32)
    tiles_per_cls = (cnt + MOE_TM - 1) // MOE_TM
    tile_end = jnp.cumsum(tiles_per_cls)
    tile_start = tile_end - tiles_per_cls
    dest = tile_start[cls] * MOE_TM + rank
    slot = jnp.arange(n_tiles * MOE_TM, dtype=jnp.int32)
    spare_row = t + ((slot // MOE_TM) % 2) * MOE_TM + slot % MOE_TM
    tok_of_slot = spare_row.at[dest].set(jnp.arange(t, dtype=jnp.int32))
    tile_id = jnp.arange(n_tiles, dtype=jnp.int32)
    n_used = tile_end[-1]
    tile_cls = jnp.minimum(jnp.searchsorted(tile_end, jnp.minimum(tile_id, n_used - 1), side="right"),
                           N_CLASSES - 1).astype(jnp.int32)
    tile_nv = jnp.where(tile_id < n_used,
                        jnp.clip(cnt[tile_cls] - (tile_id - tile_start[tile_cls]) * MOE_TM, 0, MOE_TM), 0)
    pair_lo = jnp.asarray([p[0] for p in PAIRS], jnp.int32)
    pair_hi = jnp.asarray([p[1] for p in PAIRS], jnp.int32)
    grp_base = (tile_cls // len(PAIRS)) * EXPERTS_PER_GROUP
    tile_ea = grp_base + pair_lo[tile_cls % len(PAIRS)]
    tile_eb = grp_base + pair_hi[tile_cls % len(PAIRS)]
    return tok_of_slot, tile_ea, tile_eb, jnp.concatenate([tile_nv, n_used[None]]).astype(jnp.int32)


def _cast_kernel(w_ref, o_ref):
    o_ref[0] = w_ref[0, 0].astype(o_ref.dtype)


def _expert_weights_bf16(w):
    depth, n_exp, kk, nn = w.shape
    return pl.pallas_call(
        _cast_kernel,
        grid=(depth * n_exp,),
        in_specs=[pl.BlockSpec((1, 1, kk, nn), lambda i: (i // n_exp, i % n_exp, 0, 0))],
        out_specs=pl.BlockSpec((1, kk, nn), lambda i: (i, 0, 0)),
        out_shape=jax.ShapeDtypeStruct((depth * n_exp, kk, nn), BF16),
        compiler_params=_cparams(("parallel",)),
        name="expert_weight_cast",
    )(w)


def _rearranged_w_in(w):
    scale = HEAD_DIM ** -0.5
    s = np.cumsum([ATTN_WIDTH, ATTN_WIDTH, ATTN_WIDTH, SSD_WIDTH, CONV_CH, N_SSD_HEADS]).tolist()
    q, k, v, z, xbc, dt, qm = (w[:, a:b] for a, b in zip([0] + s, s + [w.shape[1]]))
    dt = jnp.pad(dt, ((0, 0), (0, LANES - N_SSD_HEADS)))
    return jnp.concatenate([q * scale, k, v, z, xbc, qm * scale, dt], axis=1).astype(BF16)


def kernel(x, mem, w_in, w_out, rel_bias, conv_w, conv_b, dt_bias, a_log, d_skip, ssd_norm_w, w_mem_kv, mem_bias,
           ln1_g, ln1_b, ln2_g, ln2_b, w_router, b_router, w_gate, w_up, w_down):
    batch, seq, _ = x.shape
    t = batch * seq
    assert seq % (16 * ATTN_BLK) == 0 and seq % SSD_TM == 0 and t % MOE_TM == 0
    bias = _attn_bias_tables(rel_bias)
    tri = jnp.asarray(np.triu(np.ones((OUT_TM, OUT_TM), np.float32), k=1), BF16)
    wr_t = w_router.astype(F32).T
    br_col = b_router.astype(F32)[:, None]
    wg, wu, wd = (_expert_weights_bf16(w) for w in (w_gate, w_up, w_down))
    h = x.reshape(t, D_MODEL)
    for l in range(DEPTH):
        q, k, v, z, xbc, qm, dt = _inproj(h, _rearranged_w_in(w_in[l]), t)
        attn = _attention(q, k, v, bias, batch, seq)
        ssd = _ssd(z, xbc, dt, conv_w[l], conv_b[l], dt_bias[l], a_log[l], d_skip[l], ssd_norm_w[l], batch, seq)
        memo = _memory_attention(qm, mem, w_mem_kv[l].astype(BF16), mem_bias[l], batch, seq)
        x1, route, counts = _outproj_route(attn, ssd, memo, h, w_out[l].astype(BF16),
                                           ln1_g[l][None, :], ln1_b[l][None, :], wr_t, br_col, tri, t)
        tok_of_slot, tile_ea, tile_eb, tile_nv = _routing_tables(route, counts, t)
        h = _moe(x1, tok_of_slot, tile_ea + l * N_EXPERTS, tile_eb + l * N_EXPERTS, tile_nv,
                 wg, wu, wd, ln2_g[l][None, :], ln2_b[l][None, :], t)
    return h[:t].reshape(batch, seq, D_MODEL)
```

```python
import functools
import math

import numpy as np
import jax
import jax.numpy as jnp
from jax import lax
from jax.experimental import pallas as pl
from jax.experimental.pallas import tpu as pltpu

F32 = jnp.float32
BF16 = jnp.bfloat16

D_MODEL = 1024
HEAD_DIM = 64
N_ATTN_HEADS = 8
ATTN_WIDTH = N_ATTN_HEADS * HEAD_DIM
DILATED_BRANCHES = ((128, 1), (512, 4), (2048, 16))
ATTN_BLK = 128
ATTN_UNROLL = 8
N_SSD_HEADS = 4
SSD_HEAD_DIM = 64
SSD_WIDTH = N_SSD_HEADS * SSD_HEAD_DIM
SSD_GROUPS = 2
SSD_STATE = 128
CONV_WIDTH = 4
SSD_CHUNK = 128
CONV_CH = SSD_WIDTH + 2 * SSD_GROUPS * SSD_STATE
N_MEM_HEADS = 4
MEM_WIDTH = N_MEM_HEADS * HEAD_DIM
N_MEM = 256
N_BUCKETS = 32
MAX_DISTANCE = 2048
N_EXPERTS = 16
N_EXPERT_GROUPS = 4
EXPERTS_PER_GROUP = 4
D_EXPERT = 1024
DEPTH = 2
ALPHA = (2 * DEPTH) ** 0.25
LN_EPS = 1e-5
RMS_EPS = 1e-5

LANES = 128
NEG = -1e30
PAIRS = ((0, 1), (0, 2), (0, 3), (1, 2), (1, 3), (2, 3))
N_CLASSES = N_EXPERT_GROUPS * len(PAIRS)
CLS_PAD = 32

_C_Q, _C_K, _C_V = 0, ATTN_WIDTH, 2 * ATTN_WIDTH
_C_Z = 3 * ATTN_WIDTH
_C_XBC = _C_Z + SSD_WIDTH
_C_QM = _C_XBC + CONV_CH
_C_DT = _C_QM + MEM_WIDTH
IN_COLS = _C_DT + LANES
ROW_TILES = D_MODEL // LANES

IN_TM = 512
MEM_TQ = 512
SSD_TM = 512
OUT_TM = 512
MOE_TM = 256
VMEM_LIMIT = 56 * 1024 * 1024


def _cparams(sem):
    return pltpu.CompilerParams(dimension_semantics=sem, vmem_limit_bytes=VMEM_LIMIT)


def _dot(a, b):
    return jnp.dot(a, b, preferred_element_type=F32)


def _dot_nt(a, b):
    return lax.dot_general(a, b, (((1,), (1,)), ((), ())), preferred_element_type=F32)


def _load_rows(ref):
    if len(ref.shape) == 2:
        return ref[...]
    return jnp.concatenate([ref[:, k, :] for k in range(ref.shape[1])], axis=1)


def _store_rows(ref, val):
    for k in range(ref.shape[1]):
        ref[:, k, :] = val[:, k * LANES:(k + 1) * LANES]


def _row_spec(tm, x):
    if x.ndim == 2:
        return pl.BlockSpec((tm, x.shape[1]), lambda i: (i, 0))
    return pl.BlockSpec((tm,) + x.shape[1:], lambda i: (i, 0, 0))


def _inproj_kernel(x_ref, w_ref, q_ref, k_ref, v_ref, z_ref, xbc_ref, qm_ref, dt_ref):
    xb = _load_rows(x_ref).astype(BF16)
    for ref, lo, hi in ((q_ref, _C_Q, _C_K), (k_ref, _C_K, _C_V), (v_ref, _C_V, _C_Z), (z_ref, _C_Z, _C_XBC),
                        (xbc_ref, _C_XBC, _C_QM), (qm_ref, _C_QM, _C_DT), (dt_ref, _C_DT, IN_COLS)):
        ref[...] = _dot(xb, w_ref[:, lo:hi]).astype(ref.dtype)


def _inproj(x, w, t):
    widths = (ATTN_WIDTH, ATTN_WIDTH, ATTN_WIDTH, SSD_WIDTH, CONV_CH, MEM_WIDTH, LANES)
    return pl.pallas_call(
        _inproj_kernel,
        grid=(t // IN_TM,),
        in_specs=[_row_spec(IN_TM, x), pl.BlockSpec((D_MODEL, IN_COLS), lambda i: (0, 0))],
        out_specs=[pl.BlockSpec((IN_TM, wd), lambda i: (i, 0)) for wd in widths],
        out_shape=[jax.ShapeDtypeStruct((t, wd), F32) for wd in widths],
        compiler_params=_cparams(("parallel",)),
        name="inproj",
    )(x, w)


def _attn_kernel(q_ref, k_ref, v_ref, bias_ref, o_ref, obuf, lbuf, *, seq):
    head0 = lax.broadcasted_iota(jnp.int32, (ATTN_BLK, LANES), 1) < HEAD_DIM
    for bi, (_, d) in enumerate(DILATED_BRANCHES):
        nb = seq // d // ATTN_BLK

        def block(idx, carry, bi=bi, d=d, nb=nb):
            r = idx // nb
            n = idx % nb
            qs = r + d * ATTN_BLK * n
            ks = r + d * ATTN_BLK * jnp.maximum(n - 1, 0)
            first = 1 - jnp.minimum(n, 1)
            qb = q_ref[0, pl.ds(qs, ATTN_BLK, stride=d), :]
            kb = k_ref[0, pl.ds(ks, 2 * ATTN_BLK, stride=d), :].astype(BF16)
            vb = v_ref[0, pl.ds(ks, 2 * ATTN_BLK, stride=d), :].astype(BF16)
            outs, lses = [], []
            for h in range(2):
                keep = head0 if h == 0 else jnp.logical_not(head0)
                qh = jnp.where(keep, qb, 0.0).astype(BF16)
                s = _dot_nt(qh, kb) + bias_ref[bi, first, h]
                m = jnp.max(s, axis=-1, keepdims=True)
                p = jnp.exp(s - m)
                den = jnp.sum(p, axis=-1, keepdims=True)
                outs.append(_dot(p.astype(BF16), vb) / den)
                lses.append(m + jnp.log(den))
            obuf[bi, pl.ds(qs, ATTN_BLK, stride=d), :] = jnp.where(head0, outs[0], outs[1])
            lbuf[bi, pl.ds(qs, ATTN_BLK, stride=d), :] = jnp.where(head0, lses[0], lses[1])
            return carry

        lax.fori_loop(0, d * nb, block, 0, unroll=ATTN_UNROLL)

    rows = 256

    def merge(i, carry):
        sl = pl.ds(pl.multiple_of(i * rows, rows), rows)
        l0, l1, l2 = lbuf[0, sl, :], lbuf[1, sl, :], lbuf[2, sl, :]
        m = jnp.maximum(jnp.maximum(l0, l1), l2)
        w0, w1, w2 = jnp.exp(l0 - m), jnp.exp(l1 - m), jnp.exp(l2 - m)
        out = (w0 * obuf[0, sl, :] + w1 * obuf[1, sl, :] + w2 * obuf[2, sl, :]) / (w0 + w1 + w2)
        o_ref[0, sl, :] = out.astype(o_ref.dtype)
        return carry

    lax.fori_loop(0, seq // rows, merge, 0)


def _attention(q, k, v, bias, batch, seq):
    q3, k3, v3 = (a.reshape(batch, seq, ATTN_WIDTH) for a in (q, k, v))
    n_pairs = ATTN_WIDTH // LANES
    blk = pl.BlockSpec((1, seq, LANES), lambda b, hp: (b, 0, hp))
    out = pl.pallas_call(
        functools.partial(_attn_kernel, seq=seq),
        grid=(batch, n_pairs),
        in_specs=[blk, blk, blk,
                  pl.BlockSpec((3, 2, 2, ATTN_BLK, 2 * ATTN_BLK), lambda b, hp: (0, 0, hp, 0, 0))],
        out_specs=blk,
        out_shape=jax.ShapeDtypeStruct((batch, seq, ATTN_WIDTH), BF16),
        scratch_shapes=[pltpu.VMEM((3, seq, LANES), F32), pltpu.VMEM((3, seq, LANES), F32)],
        compiler_params=_cparams(("parallel", "parallel")),
        name="dilated_attn",
    )(q3, k3, v3, bias)
    return out.reshape(batch * seq, ATTN_WIDTH)


def _t5_bucket_np(dist):
    n = np.maximum(dist, 0)
    max_exact = N_BUCKETS // 2
    nf = np.maximum(n, max_exact).astype(np.float32)
    large = max_exact + (np.log(nf / np.float32(max_exact)) / np.float32(math.log(MAX_DISTANCE / max_exact))
                         * np.float32(N_BUCKETS - max_exact)).astype(np.int32)
    large = np.minimum(large, N_BUCKETS - 1)
    return np.where(n < max_exact, n, large)


def _attn_bias_tables(rel_bias):
    period = 3 * ATTN_BLK
    steps = np.arange(ATTN_BLK + 1)
    tables = []
    for _, d in DILATED_BRANCHES:
        onehot = np.eye(N_BUCKETS, dtype=np.float32)[_t5_bucket_np(steps * d)]
        tbl = jnp.dot(jnp.asarray(onehot), rel_bias.astype(F32), precision=lax.Precision.HIGHEST)
        masked = lambda n: jnp.full((n, N_ATTN_HEADS), NEG, F32)
        v_full = jnp.concatenate([tbl[::-1], masked(period - ATTN_BLK - 1)], axis=0)
        v_first = jnp.concatenate([tbl[:1], masked(period - ATTN_BLK), tbl[1:ATTN_BLK][::-1]], axis=0)
        variants = []
        for vec in (v_full, v_first):
            flat = jnp.tile(vec, (ATTN_BLK, 1))[:ATTN_BLK * (period - 1)]
            mat = flat.reshape(ATTN_BLK, period - 1, N_ATTN_HEADS)[:, :2 * ATTN_BLK]
            variants.append(mat.transpose(2, 0, 1))
        tables.append(jnp.stack(variants))
    return jnp.stack(tables)


def _per_head_lanes(cols, lane_idx):
    out = cols[:, N_SSD_HEADS - 1:N_SSD_HEADS]
    for h in range(N_SSD_HEADS - 2, -1, -1):
        out = jnp.where(lane_idx < (h + 1) * SSD_HEAD_DIM, cols[:, h:h + 1], out)
    return out


def _ssd_kernel(z_ref, xbc_ref, dt_ref, cw_ref, cb_ref, dtb_ref, a_ref, dsk_ref, nw_ref, o_ref,
                ext_ref, state_ref):
    j = pl.program_id(1)
    tm = xbc_ref.shape[0]
    halo = 8

    @pl.when(j == 0)
    def _():
        ext_ref[0:halo, :] = jnp.zeros((halo, CONV_CH), F32)
        state_ref[...] = jnp.zeros_like(state_ref)

    ext_ref[halo:halo + tm, :] = xbc_ref[...]

    c = SSD_CHUNK
    row_i = lax.broadcasted_iota(jnp.int32, (c, c), 0)
    col_i = lax.broadcasted_iota(jnp.int32, (c, c), 1)
    tril = row_i >= col_i
    tril_f = tril.astype(F32)
    lane256 = lax.broadcasted_iota(jnp.int32, (c, SSD_WIDTH), 1)
    head_even = lax.broadcasted_iota(jnp.int32, (c, LANES), 1) < SSD_HEAD_DIM
    a_row = -jnp.exp(a_ref[...])

    for ci in range(tm // c):
        base = halo + ci * c
        acc = cb_ref[...]
        for kk in range(CONV_WIDTH):
            shift = CONV_WIDTH - 1 - kk
            acc = acc + cw_ref[kk:kk + 1, :] * ext_ref[base - shift:base - shift + c, :]
        xc = acc * jax.nn.sigmoid(acc)
        xs = xc[:, :SSD_WIDTH]
        bm = xc[:, SSD_WIDTH:SSD_WIDTH + SSD_GROUPS * SSD_STATE]
        cm = xc[:, SSD_WIDTH + SSD_GROUPS * SSD_STATE:]

        dt = jax.nn.softplus(dt_ref[ci * c:(ci + 1) * c, :] + dtb_ref[...])
        adt = dt * a_row
        acs = jnp.dot(tril_f, adt, precision=lax.Precision.HIGHEST, preferred_element_type=F32)
        acs_t = acs.T
        a_last = acs[c - 1:c, :]

        dt_b = _per_head_lanes(dt, lane256)
        acs_b = _per_head_lanes(acs, lane256)
        alast_b = _per_head_lanes(a_last, lane256[:1])
        xdt = xs * dt_b
        xdt_bf = xdt.astype(BF16)
        xdec_bf = (xdt * jnp.exp(alast_b - acs_b)).astype(BF16)

        y_pairs, new_states = [], []
        for g in range(SSD_GROUPS):
            sl = slice(g * LANES, (g + 1) * LANES)
            bg = bm[:, sl]
            cg_bf = cm[:, sl].astype(BF16)
            gmat = _dot_nt(cg_bf, bg.astype(BF16))
            ys = []
            for hh in range(2):
                h = 2 * g + hh
                diff = acs[:, h:h + 1] - acs_t[h:h + 1, :]
                lmat = jnp.exp(jnp.where(tril, diff, NEG))
                ys.append(_dot((gmat * lmat).astype(BF16), xdt_bf[:, sl]))
            y_diag = jnp.where(head_even, ys[0], ys[1])
            y_off = _dot(cg_bf, state_ref[:, sl].astype(BF16))
            y_pairs.append((y_diag, y_off))
            new_states.append(_dot(bg.T.astype(BF16), xdec_bf[:, sl]))
        y_diag = jnp.concatenate([y_pairs[0][0], y_pairs[1][0]], axis=1)
        y_off = jnp.concatenate([y_pairs[0][1], y_pairs[1][1]], axis=1)
        state_ref[...] = state_ref[...] * jnp.exp(alast_b) + jnp.concatenate(new_states, axis=1)

        y = y_diag + y_off * jnp.exp(acs_b) + dsk_ref[...] * xs
        zc = z_ref[ci * c:(ci + 1) * c, :]
        y = y * (zc * jax.nn.sigmoid(zc))
        y = y * lax.rsqrt(jnp.mean(jnp.square(y), axis=-1, keepdims=True) + RMS_EPS) * nw_ref[...]
        o_ref[ci * c:(ci + 1) * c, :] = y.astype(o_ref.dtype)

    ext_ref[0:halo, :] = ext_ref[tm:tm + halo, :]


def _ssd(z, xbc, dt, conv_w, conv_b, dt_bias, a_log, d_skip, norm_w, batch, seq):
    t = z.shape[0]
    steps = seq // SSD_TM
    pad = lambda a: jnp.pad(a.astype(F32), (0, LANES - a.shape[0]))[None, :]
    row = lambda n: pl.BlockSpec((SSD_TM, n), lambda b, j: (b * steps + j, 0))
    const = lambda shape: pl.BlockSpec(shape, lambda b, j: (0, 0))
    return pl.pallas_call(
        _ssd_kernel,
        grid=(batch, steps),
        in_specs=[row(SSD_WIDTH), row(CONV_CH), row(LANES),
                  const((CONV_WIDTH, CONV_CH)), const((1, CONV_CH)), const((1, LANES)), const((1, LANES)),
                  const((1, SSD_WIDTH)), const((1, SSD_WIDTH))],
        out_specs=row(SSD_WIDTH),
        out_shape=jax.ShapeDtypeStruct((t, SSD_WIDTH), BF16),
        scratch_shapes=[pltpu.VMEM((SSD_TM + 8, CONV_CH), F32), pltpu.VMEM((SSD_STATE, SSD_WIDTH), F32)],
        compiler_params=_cparams(("parallel", "arbitrary")),
        name="ssd",
    )(z, xbc, dt, conv_w.astype(F32), conv_b.astype(F32)[None, :], pad(dt_bias), pad(a_log),
      jnp.repeat(d_skip.astype(F32), SSD_HEAD_DIM)[None, :], norm_w.astype(F32)[None, :])


def _mem_kernel(qm_ref, mem_ref, wkv_ref, mb_ref, o_ref, k_sc, v_sc):
    @pl.when(pl.program_id(1) == 0)
    def _():
        kv = _dot(mem_ref[0].astype(BF16), wkv_ref[...])
        k_sc[...] = kv[:, :MEM_WIDTH].astype(BF16)
        v_sc[...] = kv[:, MEM_WIDTH:].astype(BF16)

    tq = qm_ref.shape[0]
    head0 = lax.broadcasted_iota(jnp.int32, (tq, LANES), 1) < HEAD_DIM
    outs = []
    for hp in range(MEM_WIDTH // LANES):
        sl = slice(hp * LANES, (hp + 1) * LANES)
        qp = qm_ref[:, sl]
        kp = k_sc[:, sl]
        vp = v_sc[:, sl]
        res = []
        for hh in range(2):
            h = 2 * hp + hh
            keep = head0 if hh == 0 else jnp.logical_not(head0)
            qh = jnp.where(keep, qp, 0.0).astype(BF16)
            s = _dot_nt(qh, kp) + mb_ref[h:h + 1, :]
            m = jnp.max(s, axis=-1, keepdims=True)
            p = jnp.exp(s - m)
            den = jnp.sum(p, axis=-1, keepdims=True)
            res.append(_dot(p.astype(BF16), vp) / den)
        outs.append(jnp.where(head0, res[0], res[1]))
    o_ref[...] = jnp.concatenate(outs, axis=1).astype(o_ref.dtype)


def _memory_attention(qm, mem, wkv, mem_bias, batch, seq):
    t = qm.shape[0]
    steps = seq // MEM_TQ
    mb = jnp.pad(mem_bias.astype(F32), ((0, 8 - N_MEM_HEADS), (0, 0)))
    return pl.pallas_call(
        _mem_kernel,
        grid=(batch, steps),
        in_specs=[pl.BlockSpec((MEM_TQ, MEM_WIDTH), lambda b, j: (b * steps + j, 0)),
                  pl.BlockSpec((1, N_MEM, D_MODEL), lambda b, j: (b, 0, 0)),
                  pl.BlockSpec((D_MODEL, 2 * MEM_WIDTH), lambda b, j: (0, 0)),
                  pl.BlockSpec((8, N_MEM), lambda b, j: (0, 0))],
        out_specs=pl.BlockSpec((MEM_TQ, MEM_WIDTH), lambda b, j: (b * steps + j, 0)),
        out_shape=jax.ShapeDtypeStruct((t, MEM_WIDTH), BF16),
        scratch_shapes=[pltpu.VMEM((N_MEM, MEM_WIDTH), BF16), pltpu.VMEM((N_MEM, MEM_WIDTH), BF16)],
        compiler_params=_cparams(("parallel", "arbitrary")),
        name="mem_attn",
    )(qm, mem, wkv, mb)


def _layer_norm(h, g, b):
    mu = jnp.mean(h, axis=-1, keepdims=True)
    var = jnp.mean(jnp.square(h - mu), axis=-1, keepdims=True)
    return (h - mu) * lax.rsqrt(var + LN_EPS) * g + b


def _outproj_kernel(attn_ref, ssd_ref, memo_ref, x_ref, wo_ref, g_ref, b_ref, wr_ref, br_ref, tri_ref,
                    x1_ref, route_ref, cnt_ref, carry_ref):
    i = pl.program_id(0)

    @pl.when(i == 0)
    def _():
        carry_ref[...] = jnp.zeros_like(carry_ref)

    y = (_dot(attn_ref[...], wo_ref[0:ATTN_WIDTH, :])
         + _dot(ssd_ref[...], wo_ref[ATTN_WIDTH:ATTN_WIDTH + SSD_WIDTH, :])
         + _dot(memo_ref[...], wo_ref[ATTN_WIDTH + SSD_WIDTH:, :]))
    x1 = _layer_norm(ALPHA * _load_rows(x_ref) + y, g_ref[...], b_ref[...])
    _store_rows(x1_ref, x1)

    logits = lax.dot_general(wr_ref[...], x1, (((1,), (1,)), ((), ())), precision=lax.Precision.HIGHEST,
                             preferred_element_type=F32) + br_ref[...]
    mx = jnp.max(logits, axis=0, keepdims=True)
    ex = jnp.exp(logits - mx)
    sc = ex / jnp.sum(ex, axis=0, keepdims=True)
    s = [sc[e:e + 1, :] for e in range(N_EXPERTS)]

    def top2_sum(vals):
        best = None
        for a, b in PAIRS:
            pair = vals[a] + vals[b]
            best = pair if best is None else jnp.maximum(best, pair)
        return best

    grp = [top2_sum(s[4 * g:4 * g + 4]) for g in range(N_EXPERT_GROUPS)]
    best_g = jnp.zeros_like(grp[0], dtype=jnp.int32)
    best_v = grp[0]
    for g in range(1, N_EXPERT_GROUPS):
        upd = grp[g] > best_v
        best_g = jnp.where(upd, g, best_g)
        best_v = jnp.where(upd, grp[g], best_v)
    sel = []
    for jj in range(EXPERTS_PER_GROUP):
        v = s[3 * EXPERTS_PER_GROUP + jj]
        for g in range(N_EXPERT_GROUPS - 2, -1, -1):
            v = jnp.where(best_g == g, s[g * EXPERTS_PER_GROUP + jj], v)
        sel.append(v)

    def argmax_first(vals):
        idx = jnp.zeros_like(best_g)
        val = vals[0]
        for jj in range(1, len(vals)):
            upd = vals[jj] > val
            idx = jnp.where(upd, jj, idx)
            val = jnp.where(upd, vals[jj], val)
        return idx, val

    i1, _ = argmax_first(sel)
    i2, _ = argmax_first([jnp.where(i1 == jj, -1.0, sel[jj]) for jj in range(EXPERTS_PER_GROUP)])
    lo = jnp.minimum(i1, i2)
    hi = jnp.maximum(i1, i2)
    pair_idx = jnp.where(lo == 0, hi - 1, jnp.where(lo == 1, hi + 1, 5))
    cls = best_g * len(PAIRS) + pair_idx

    tm = cls.shape[1]
    onehot = (lax.broadcasted_iota(jnp.int32, (CLS_PAD, tm), 0) == cls)
    prefix = _dot(onehot.astype(BF16), tri_ref[...])
    oh = onehot.astype(F32)
    carry = carry_ref[:, 0:1]
    rank = jnp.sum(oh * (prefix + carry), axis=0, keepdims=True)
    new_carry = carry + jnp.sum(oh, axis=1, keepdims=True)
    carry_ref[...] = jnp.broadcast_to(new_carry, carry_ref.shape)
    cnt_ref[...] = jnp.broadcast_to(new_carry, cnt_ref.shape)

    route_ref[...] = jnp.concatenate([cls.astype(F32), rank, jnp.zeros((6, tm), F32)], axis=0)


def _outproj_route(attn, ssd, memo, x, wo, g, b, wr_t, br_col, tri, t):
    row = lambda n: pl.BlockSpec((OUT_TM, n), lambda i: (i, 0))
    const = lambda shape: pl.BlockSpec(shape, lambda i: (0, 0))
    return pl.pallas_call(
        _outproj_kernel,
        grid=(t // OUT_TM,),
        in_specs=[row(ATTN_WIDTH), row(SSD_WIDTH), row(MEM_WIDTH), _row_spec(OUT_TM, x),
                  const((D_MODEL, D_MODEL)), const((1, D_MODEL)), const((1, D_MODEL)),
                  const((N_EXPERTS, D_MODEL)), const((N_EXPERTS, 1)), const((OUT_TM, OUT_TM))],
        out_specs=[pl.BlockSpec((OUT_TM, ROW_TILES, LANES), lambda i: (i, 0, 0)),
                   pl.BlockSpec((8, OUT_TM), lambda i: (0, i)), const((CLS_PAD, LANES))],
        out_shape=[jax.ShapeDtypeStruct((t, ROW_TILES, LANES), F32), jax.ShapeDtypeStruct((8, t), F32),
                   jax.ShapeDtypeStruct((CLS_PAD, LANES), F32)],
        scratch_shapes=[pltpu.VMEM((CLS_PAD, LANES), F32)],
        compiler_params=_cparams(("arbitrary",)),
        name="outproj_route",
    )(attn, ssd, memo, x, wo, g, b, wr_t, br_col, tri)


def _moe_kernel(ea_ref, eb_ref, nv_ref, tok_ref,
                x1_hbm, wga, wua, wda, wgb, wub, wdb, g_ref, b_ref, wr_ref, br_ref,
                out_hbm, xbuf0, xbuf1, obuf0, obuf1, gsem, ssem, *, n_tok, first_expert):
    i = pl.program_id(0)
    n_used = nv_ref[pl.num_programs(0)]
    xbufs = (xbuf0, xbuf1)
    obufs = (obuf0, obuf1)

    def gather_copy(tok, r, p):
        return pltpu.make_async_copy(x1_hbm.at[jnp.minimum(tok, n_tok - 1)], xbufs[p].at[r], gsem.at[p])

    def scatter_copy(dst, r, p):
        return pltpu.make_async_copy(obufs[p].at[r], out_hbm.at[dst], ssem.at[p])

    def tile_gather_done(p):
        return pltpu.make_async_copy(x1_hbm.at[pl.ds(0, MOE_TM)], xbufs[p], gsem.at[p])

    def tile_scatter_done(p):
        return pltpu.make_async_copy(obufs[p], out_hbm.at[pl.ds(0, MOE_TM)], ssem.at[p])

    @pl.when(i == 0)
    def _():
        for p in range(2):
            obufs[p][...] = jnp.zeros_like(obufs[p])
            spare = pltpu.make_async_copy(obufs[p], out_hbm.at[pl.ds(n_tok + p * MOE_TM, MOE_TM)], ssem.at[p])
            spare.start()
            spare.wait()

        def first(r, carry):
            gather_copy(tok_ref[r], r, 0).start()
            return carry

        lax.fori_loop(0, MOE_TM, first, 0)

    def step(p):
        q = 1 - p
        tile_gather_done(p).wait()
        next_base = jnp.minimum(i + 1, n_used - 1) * MOE_TM
        prev_base = jnp.maximum(i - 1, 0) * MOE_TM
        is_first = i == 0

        def issue_rows(lo, hi):
            for r in range(lo, hi):
                gather_copy(tok_ref[next_base + r], r, q).start()
                dst = jnp.where(is_first, n_tok + q * MOE_TM + r, tok_ref[prev_base + r])
                scatter_copy(dst, r, q).start()

        x = _load_rows(xbufs[p])
        xb = x.astype(BF16)
        ea = ea_ref[i] - first_expert
        eb = eb_ref[i] - first_expert
        w_diff = wr_ref[pl.ds(eb, 1), :] - wr_ref[pl.ds(ea, 1), :]
        logit_diff = (jnp.sum(x * w_diff, axis=-1, keepdims=True)
                      + (br_ref[pl.ds(eb, 1), :] - br_ref[pl.ds(ea, 1), :]))
        gates = (jax.nn.sigmoid(-logit_diff), jax.nn.sigmoid(logit_diff))
        half = D_EXPERT // 2
        rows_per_chunk = MOE_TM // 4
        y = None
        for e, (wg, wu, wd) in enumerate(((wga, wua, wda), (wgb, wub, wdb))):
            acc = None
            for c in range(2):
                chunk = 2 * e + c
                issue_rows(chunk * rows_per_chunk, (chunk + 1) * rows_per_chunk)
                cs = slice(c * half, (c + 1) * half)
                gate = _dot(xb, wg[0, :, cs])
                up = _dot(xb, wu[0, :, cs])
                hid = (gate * jax.nn.sigmoid(gate) * up).astype(BF16)
                part = _dot(hid, wd[0, cs, :])
                acc = part if acc is None else acc + part
            term = gates[e] * acc
            y = term if y is None else y + term
        out = _layer_norm(ALPHA * x + y, g_ref[...], b_ref[...])
        tile_scatter_done(q).wait()
        _store_rows(obufs[p], out)

        @pl.when(i == n_used - 1)
        def _():
            base = i * MOE_TM

            def last(r, carry):
                scatter_copy(tok_ref[base + r], r, p).start()
                return carry

            lax.fori_loop(0, MOE_TM, last, 0)
            tile_scatter_done(p).wait()
            tile_gather_done(q).wait()

    for p in range(2):
        @pl.when(jnp.logical_and(i < n_used, i % 2 == p))
        def _(p=p):
            step(p)


def _moe(x1, tok_of_slot, tile_ea, tile_eb, tile_nv, wg, wu, wd, g, b, wr_t, br_col, n_tok, first_expert):
    nt = tile_ea.shape[0]
    wspec = lambda which: pl.BlockSpec(
        (1, D_MODEL, D_EXPERT), lambda i, ea, eb, nv, tok: ((ea if which == 0 else eb)[i], 0, 0))
    const = lambda shape: pl.BlockSpec(shape, lambda i, ea, eb, nv, tok: (0, 0))
    tile_buf = pltpu.VMEM((MOE_TM, ROW_TILES, LANES), F32)
    grid_spec = pltpu.PrefetchScalarGridSpec(
        num_scalar_prefetch=4,
        grid=(nt,),
        in_specs=[pl.BlockSpec(memory_space=pl.ANY),
                  wspec(0), wspec(0), wspec(0), wspec(1), wspec(1), wspec(1),
                  const((1, D_MODEL)), const((1, D_MODEL)), const((N_EXPERTS, D_MODEL)), const((N_EXPERTS, 1))],
        out_specs=pl.BlockSpec(memory_space=pl.ANY),
        scratch_shapes=[tile_buf, tile_buf, tile_buf, tile_buf,
                        pltpu.SemaphoreType.DMA((2,)), pltpu.SemaphoreType.DMA((2,))],
    )
    return pl.pallas_call(
        functools.partial(_moe_kernel, n_tok=n_tok, first_expert=first_expert),
        grid_spec=grid_spec,
        out_shape=jax.ShapeDtypeStruct((n_tok + 2 * MOE_TM, ROW_TILES, LANES), F32),
        compiler_params=_cparams(("arbitrary",)),
        name="moe_experts",
    )(tile_ea, tile_eb, tile_nv, tok_of_slot, x1, wg, wu, wd, wg, wu, wd, g, b, wr_t, br_col)


def _routing_tables(route, counts, t):
    n_tiles = t // MOE_TM + N_CLASSES
    cls = route[0].astype(jnp.int32)
    rank = route[1].astype(jnp.int32)
    cnt = counts[:N_CLASSES, 0].astype(jnp.int32)
    tiles_per_cls = (cnt + MOE_TM - 1) // MOE_TM
    tile_end = jnp.cumsum(tiles_per_cls)
    tile_start = tile_end - tiles_per_cls
    dest = tile_start[cls] * MOE_TM + rank
    slot = jnp.arange(n_tiles * MOE_TM, dtype=jnp.int32)
    spare_row = t + ((slot // MOE_TM) % 2) * MOE_TM + slot % MOE_TM
    tok_of_slot = spare_row.at[dest].set(jnp.arange(t, dtype=jnp.int32))
    tile_id = jnp.arange(n_tiles, dtype=jnp.int32)
    n_used = tile_end[-1]
    tile_cls = jnp.minimum(jnp.searchsorted(tile_end, jnp.minimum(tile_id, n_used - 1), side="right"),
                           N_CLASSES - 1).astype(jnp.int32)
    tile_nv = jnp.where(tile_id < n_used,
                        jnp.clip(cnt[tile_cls] - (tile_id - tile_start[tile_cls]) * MOE_TM, 0, MOE_TM), 0)
    pair_lo = jnp.asarray([p[0] for p in PAIRS], jnp.int32)
    pair_hi = jnp.asarray([p[1] for p in PAIRS], jnp.int32)
    grp_base = (tile_cls // len(PAIRS)) * EXPERTS_PER_GROUP
    tile_ea = grp_base + pair_lo[tile_cls % len(PAIRS)]
    tile_eb = grp_base + pair_hi[tile_cls % len(PAIRS)]
    return tok_of_slot, tile_ea, tile_eb, jnp.concatenate([tile_nv, n_used[None]]).astype(jnp.int32)


def _cast_kernel(w_ref, o_ref):
    o_ref[0] = w_ref[0, 0].astype(o_ref.dtype)


def _expert_weights_bf16(w):
    depth, n_exp, kk, nn = w.shape
    return pl.pallas_call(
        _cast_kernel,
        grid=(depth * n_exp,),
        in_specs=[pl.BlockSpec((1, 1, kk, nn), lambda i: (i // n_exp, i % n_exp, 0, 0))],
        out_specs=pl.BlockSpec((1, kk, nn), lambda i: (i, 0, 0)),
        out_shape=jax.ShapeDtypeStruct((depth * n_exp, kk, nn), BF16),
        compiler_params=_cparams(("parallel",)),
        name="expert_weight_cast",
    )(w)


def _rearranged_w_in(w):
    scale = HEAD_DIM ** -0.5
    s = np.cumsum([ATTN_WIDTH, ATTN_WIDTH, ATTN_WIDTH, SSD_WIDTH, CONV_CH, N_SSD_HEADS]).tolist()
    q, k, v, z, xbc, dt, qm = (w[:, a:b] for a, b in zip([0] + s, s + [w.shape[1]]))
    dt = jnp.pad(dt, ((0, 0), (0, LANES - N_SSD_HEADS)))
    return jnp.concatenate([q * scale, k, v, z, xbc, qm * scale, dt], axis=1).astype(BF16)


def kernel(x, mem, w_in, w_out, rel_bias, conv_w, conv_b, dt_bias, a_log, d_skip, ssd_norm_w, w_mem_kv, mem_bias,
           ln1_g, ln1_b, ln2_g, ln2_b, w_router, b_router, w_gate, w_up, w_down):
    batch, seq, _ = x.shape
    t = batch * seq
    assert seq % (16 * ATTN_BLK) == 0 and seq % SSD_TM == 0 and t % MOE_TM == 0
    bias = _attn_bias_tables(rel_bias)
    tri = jnp.asarray(np.triu(np.ones((OUT_TM, OUT_TM), np.float32), k=1), BF16)
    wr_t = w_router.astype(F32).T
    br_col = b_router.astype(F32)[:, None]
    wg, wu, wd = (_expert_weights_bf16(w) for w in (w_gate, w_up, w_down))
    h = x.reshape(t, D_MODEL)
    for l in range(DEPTH):
        q, k, v, z, xbc, qm, dt = _inproj(h, _rearranged_w_in(w_in[l]), t)
        attn = _attention(q, k, v, bias, batch, seq)
        ssd = _ssd(z, xbc, dt, conv_w[l], conv_b[l], dt_bias[l], a_log[l], d_skip[l], ssd_norm_w[l], batch, seq)
        memo = _memory_attention(qm, mem, w_mem_kv[l].astype(BF16), mem_bias[l], batch, seq)
        x1, route, counts = _outproj_route(attn, ssd, memo, h, w_out[l].astype(BF16),
                                           ln1_g[l][None, :], ln1_b[l][None, :], wr_t, br_col, tri, t)
        tok_of_slot, tile_ea, tile_eb, tile_nv = _routing_tables(route, counts, t)
        h = _moe(x1, tok_of_slot, tile_ea + l * N_EXPERTS, tile_eb + l * N_EXPERTS, tile_nv,
                 wg, wu, wd, ln2_g[l][None, :], ln2_b[l][None, :], wr_t, br_col, t, l * N_EXPERTS)
    return h[:t].reshape(batch, seq, D_MODEL)
```

```python
import functools
import math

import numpy as np
import jax
import jax.numpy as jnp
from jax import lax
from jax.experimental import pallas as pl
from jax.experimental.pallas import tpu as pltpu

F32 = jnp.float32
BF16 = jnp.bfloat16

D_MODEL = 1024
HEAD_DIM = 64
N_ATTN_HEADS = 8
ATTN_WIDTH = N_ATTN_HEADS * HEAD_DIM
DILATED_BRANCHES = ((128, 1), (512, 4), (2048, 16))
ATTN_BLK = 128
ATTN_UNROLL = 8
N_SSD_HEADS = 4
SSD_HEAD_DIM = 64
SSD_WIDTH = N_SSD_HEADS * SSD_HEAD_DIM
SSD_GROUPS = 2
SSD_STATE = 128
CONV_WIDTH = 4
SSD_CHUNK = 128
CONV_CH = SSD_WIDTH + 2 * SSD_GROUPS * SSD_STATE
N_MEM_HEADS = 4
MEM_WIDTH = N_MEM_HEADS * HEAD_DIM
N_MEM = 256
N_BUCKETS = 32
MAX_DISTANCE = 2048
N_EXPERTS = 16
N_EXPERT_GROUPS = 4
EXPERTS_PER_GROUP = 4
D_EXPERT = 1024
DEPTH = 2
ALPHA = (2 * DEPTH) ** 0.25
LN_EPS = 1e-5
RMS_EPS = 1e-5

LANES = 128
NEG = -1e30
PAIRS = ((0, 1), (0, 2), (0, 3), (1, 2), (1, 3), (2, 3))
N_CLASSES = N_EXPERT_GROUPS * len(PAIRS)
CLS_PAD = 32

_C_Q, _C_K, _C_V = 0, ATTN_WIDTH, 2 * ATTN_WIDTH
_C_Z = 3 * ATTN_WIDTH
_C_XBC = _C_Z + SSD_WIDTH
_C_QM = _C_XBC + CONV_CH
_C_DT = _C_QM + MEM_WIDTH
IN_COLS = _C_DT + LANES
TOKEN_ROW = (D_MODEL,)

IN_TM = 512
MEM_TQ = 512
SSD_TM = 512
OUT_TM = 512
MOE_TM = 256
VMEM_LIMIT = 56 * 1024 * 1024


def _cparams(sem):
    return pltpu.CompilerParams(dimension_semantics=sem, vmem_limit_bytes=VMEM_LIMIT)


def _dot(a, b):
    return jnp.dot(a, b, preferred_element_type=F32)


def _dot_nt(a, b):
    return lax.dot_general(a, b, (((1,), (1,)), ((), ())), preferred_element_type=F32)


def _load_rows(ref):
    if len(ref.shape) == 2:
        return ref[...]
    return jnp.concatenate([ref[:, k, :] for k in range(ref.shape[1])], axis=1)


def _store_rows(ref, val):
    if len(ref.shape) == 2:
        ref[...] = val
        return
    for k in range(ref.shape[1]):
        ref[:, k, :] = val[:, k * LANES:(k + 1) * LANES]


def _row_spec(tm, x):
    if x.ndim == 2:
        return pl.BlockSpec((tm, x.shape[1]), lambda i: (i, 0))
    return pl.BlockSpec((tm,) + x.shape[1:], lambda i: (i, 0, 0))


def _inproj_kernel(x_ref, w_ref, q_ref, k_ref, v_ref, z_ref, xbc_ref, qm_ref, dt_ref):
    xb = _load_rows(x_ref).astype(BF16)
    for ref, lo, hi in ((q_ref, _C_Q, _C_K), (k_ref, _C_K, _C_V), (v_ref, _C_V, _C_Z), (z_ref, _C_Z, _C_XBC),
                        (xbc_ref, _C_XBC, _C_QM), (qm_ref, _C_QM, _C_DT), (dt_ref, _C_DT, IN_COLS)):
        ref[...] = _dot(xb, w_ref[:, lo:hi]).astype(ref.dtype)


def _inproj(x, w, t):
    widths = (ATTN_WIDTH, ATTN_WIDTH, ATTN_WIDTH, SSD_WIDTH, CONV_CH, MEM_WIDTH, LANES)
    return pl.pallas_call(
        _inproj_kernel,
        grid=(t // IN_TM,),
        in_specs=[_row_spec(IN_TM, x), pl.BlockSpec((D_MODEL, IN_COLS), lambda i: (0, 0))],
        out_specs=[pl.BlockSpec((IN_TM, wd), lambda i: (i, 0)) for wd in widths],
        out_shape=[jax.ShapeDtypeStruct((t, wd), F32) for wd in widths],
        compiler_params=_cparams(("parallel",)),
        name="inproj",
    )(x, w)


def _attn_kernel(q_ref, k_ref, v_ref, bias_ref, o_ref, obuf, mbuf, dbuf, *, seq):
    def attend(bi, d, qs, ks, qrows, bias_of_head):
        head0 = lax.broadcasted_iota(jnp.int32, (qrows, LANES), 1) < HEAD_DIM
        qb = q_ref[0, pl.ds(qs, qrows, stride=d), :]
        kb = k_ref[0, pl.ds(ks, 2 * ATTN_BLK, stride=d), :].astype(BF16)
        vb = v_ref[0, pl.ds(ks, 2 * ATTN_BLK, stride=d), :].astype(BF16)
        outs, maxes, dens = [], [], []
        for h in range(2):
            keep = head0 if h == 0 else jnp.logical_not(head0)
            qh = jnp.where(keep, qb, 0.0).astype(BF16)
            s = _dot_nt(qh, kb) + bias_of_head(h)
            m = jnp.max(s, axis=-1, keepdims=True)
            p = jnp.exp(s - m)
            outs.append(_dot(p.astype(BF16), vb))
            maxes.append(m)
            dens.append(jnp.sum(p, axis=-1, keepdims=True))
        rows = pl.ds(qs, qrows, stride=d)
        obuf[bi, rows, :] = jnp.where(head0, outs[0], outs[1])
        mbuf[bi, rows, :] = jnp.where(head0, maxes[0], maxes[1])
        dbuf[bi, rows, :] = jnp.where(head0, dens[0], dens[1])

    for bi, (_, d) in enumerate(DILATED_BRANCHES):
        nb = seq // d // ATTN_BLK
        if nb == 2:
            def pair(r, carry, bi=bi, d=d):
                attend(bi, d, r, r, 2 * ATTN_BLK,
                       lambda h: jnp.concatenate([bias_ref[bi, 1, h], bias_ref[bi, 0, h]], axis=0))
                return carry

            lax.fori_loop(0, d, pair, 0, unroll=ATTN_UNROLL // 2)
            continue

        def block(idx, carry, bi=bi, d=d, nb=nb):
            r = idx // nb
            n = idx % nb
            first = 1 - jnp.minimum(n, 1)
            attend(bi, d, r + d * ATTN_BLK * n, r + d * ATTN_BLK * jnp.maximum(n - 1, 0), ATTN_BLK,
                   lambda h: bias_ref[bi, first, h])
            return carry

        lax.fori_loop(0, d * nb, block, 0, unroll=ATTN_UNROLL)

    rows = 256

    def merge(i, carry):
        sl = pl.ds(pl.multiple_of(i * rows, rows), rows)
        m0, m1, m2 = mbuf[0, sl, :], mbuf[1, sl, :], mbuf[2, sl, :]
        m = jnp.maximum(jnp.maximum(m0, m1), m2)
        w0, w1, w2 = jnp.exp(m0 - m), jnp.exp(m1 - m), jnp.exp(m2 - m)
        num = w0 * obuf[0, sl, :] + w1 * obuf[1, sl, :] + w2 * obuf[2, sl, :]
        den = w0 * dbuf[0, sl, :] + w1 * dbuf[1, sl, :] + w2 * dbuf[2, sl, :]
        o_ref[0, sl, :] = (num / den).astype(o_ref.dtype)
        return carry

    lax.fori_loop(0, seq // rows, merge, 0)


def _attention(q, k, v, bias, batch, seq):
    q3, k3, v3 = (a.reshape(batch, seq, ATTN_WIDTH) for a in (q, k, v))
    n_pairs = ATTN_WIDTH // LANES
    blk = pl.BlockSpec((1, seq, LANES), lambda b, hp: (b, 0, hp))
    out = pl.pallas_call(
        functools.partial(_attn_kernel, seq=seq),
        grid=(batch, n_pairs),
        in_specs=[blk, blk, blk,
                  pl.BlockSpec((3, 2, 2, ATTN_BLK, 2 * ATTN_BLK), lambda b, hp: (0, 0, hp, 0, 0))],
        out_specs=blk,
        out_shape=jax.ShapeDtypeStruct((batch, seq, ATTN_WIDTH), BF16),
        scratch_shapes=[pltpu.VMEM((3, seq, LANES), F32)] * 3,
        compiler_params=_cparams(("parallel", "parallel")),
        name="dilated_attn",
    )(q3, k3, v3, bias)
    return out.reshape(batch * seq, ATTN_WIDTH)


def _t5_bucket_np(dist):
    n = np.maximum(dist, 0)
    max_exact = N_BUCKETS // 2
    nf = np.maximum(n, max_exact).astype(np.float32)
    large = max_exact + (np.log(nf / np.float32(max_exact)) / np.float32(math.log(MAX_DISTANCE / max_exact))
                         * np.float32(N_BUCKETS - max_exact)).astype(np.int32)
    large = np.minimum(large, N_BUCKETS - 1)
    return np.where(n < max_exact, n, large)


def _attn_bias_tables(rel_bias):
    period = 3 * ATTN_BLK
    steps = np.arange(ATTN_BLK + 1)
    tables = []
    for _, d in DILATED_BRANCHES:
        onehot = np.eye(N_BUCKETS, dtype=np.float32)[_t5_bucket_np(steps * d)]
        tbl = jnp.dot(jnp.asarray(onehot), rel_bias.astype(F32), precision=lax.Precision.HIGHEST)
        masked = lambda n: jnp.full((n, N_ATTN_HEADS), NEG, F32)
        v_full = jnp.concatenate([tbl[::-1], masked(period - ATTN_BLK - 1)], axis=0)
        v_first = jnp.concatenate([tbl[:1], masked(period - ATTN_BLK), tbl[1:ATTN_BLK][::-1]], axis=0)
        variants = []
        for vec in (v_full, v_first):
            flat = jnp.tile(vec, (ATTN_BLK, 1))[:ATTN_BLK * (period - 1)]
            mat = flat.reshape(ATTN_BLK, period - 1, N_ATTN_HEADS)[:, :2 * ATTN_BLK]
            variants.append(mat.transpose(2, 0, 1))
        tables.append(jnp.stack(variants))
    return jnp.stack(tables)


def _per_head_lanes(cols, lane_idx):
    out = cols[:, N_SSD_HEADS - 1:N_SSD_HEADS]
    for h in range(N_SSD_HEADS - 2, -1, -1):
        out = jnp.where(lane_idx < (h + 1) * SSD_HEAD_DIM, cols[:, h:h + 1], out)
    return out


def _ssd_kernel(z_ref, xbc_ref, dt_ref, cw_ref, cb_ref, dtb_ref, a_ref, dsk_ref, nw_ref, o_ref,
                ext_ref, state_ref):
    j = pl.program_id(1)
    tm = xbc_ref.shape[0]
    halo = 8

    @pl.when(j == 0)
    def _():
        ext_ref[0:halo, :] = jnp.zeros((halo, CONV_CH), F32)
        state_ref[...] = jnp.zeros_like(state_ref)

    ext_ref[halo:halo + tm, :] = xbc_ref[...]

    c = SSD_CHUNK
    row_i = lax.broadcasted_iota(jnp.int32, (c, c), 0)
    col_i = lax.broadcasted_iota(jnp.int32, (c, c), 1)
    tril = row_i >= col_i
    tril_f = tril.astype(F32)
    lane256 = lax.broadcasted_iota(jnp.int32, (c, SSD_WIDTH), 1)
    head_even = lax.broadcasted_iota(jnp.int32, (c, LANES), 1) < SSD_HEAD_DIM
    a_row = -jnp.exp(a_ref[...])

    for ci in range(tm // c):
        base = halo + ci * c
        acc = cb_ref[...]
        for kk in range(CONV_WIDTH):
            shift = CONV_WIDTH - 1 - kk
            acc = acc + cw_ref[kk:kk + 1, :] * ext_ref[base - shift:base - shift + c, :]
        xc = acc * jax.nn.sigmoid(acc)
        xs = xc[:, :SSD_WIDTH]
        bm = xc[:, SSD_WIDTH:SSD_WIDTH + SSD_GROUPS * SSD_STATE]
        cm = xc[:, SSD_WIDTH + SSD_GROUPS * SSD_STATE:]

        dt = jax.nn.softplus(dt_ref[ci * c:(ci + 1) * c, :] + dtb_ref[...])
        adt = dt * a_row
        acs = jnp.dot(tril_f, adt, precision=lax.Precision.HIGHEST, preferred_element_type=F32)
        acs_t = acs.T
        a_last = acs[c - 1:c, :]

        dt_b = _per_head_lanes(dt, lane256)
        acs_b = _per_head_lanes(acs, lane256)
        alast_b = _per_head_lanes(a_last, lane256[:1])
        xdt = xs * dt_b
        xdt_bf = xdt.astype(BF16)
        xdec_bf = (xdt * jnp.exp(alast_b - acs_b)).astype(BF16)

        y_pairs, new_states = [], []
        for g in range(SSD_GROUPS):
            sl = slice(g * LANES, (g + 1) * LANES)
            bg = bm[:, sl]
            cg_bf = cm[:, sl].astype(BF16)
            gmat = _dot_nt(cg_bf, bg.astype(BF16))
            ys = []
            for hh in range(2):
                h = 2 * g + hh
                diff = acs[:, h:h + 1] - acs_t[h:h + 1, :]
                lmat = jnp.exp(jnp.where(tril, diff, NEG))
                ys.append(_dot((gmat * lmat).astype(BF16), xdt_bf[:, sl]))
            y_diag = jnp.where(head_even, ys[0], ys[1])
            y_off = _dot(cg_bf, state_ref[:, sl].astype(BF16))
            y_pairs.append((y_diag, y_off))
            new_states.append(_dot(bg.T.astype(BF16), xdec_bf[:, sl]))
        y_diag = jnp.concatenate([y_pairs[0][0], y_pairs[1][0]], axis=1)
        y_off = jnp.concatenate([y_pairs[0][1], y_pairs[1][1]], axis=1)
        state_ref[...] = state_ref[...] * jnp.exp(alast_b) + jnp.concatenate(new_states, axis=1)

        y = y_diag + y_off * jnp.exp(acs_b) + dsk_ref[...] * xs
        zc = z_ref[ci * c:(ci + 1) * c, :]
        y = y * (zc * jax.nn.sigmoid(zc))
        y = y * lax.rsqrt(jnp.mean(jnp.square(y), axis=-1, keepdims=True) + RMS_EPS) * nw_ref[...]
        o_ref[ci * c:(ci + 1) * c, :] = y.astype(o_ref.dtype)

    ext_ref[0:halo, :] = ext_ref[tm:tm + halo, :]


def _ssd(z, xbc, dt, conv_w, conv_b, dt_bias, a_log, d_skip, norm_w, batch, seq):
    t = z.shape[0]
    steps = seq // SSD_TM
    pad = lambda a: jnp.pad(a.astype(F32), (0, LANES - a.shape[0]))[None, :]
    row = lambda n: pl.BlockSpec((SSD_TM, n), lambda b, j: (b * steps + j, 0))
    const = lambda shape: pl.BlockSpec(shape, lambda b, j: (0, 0))
    return pl.pallas_call(
        _ssd_kernel,
        grid=(batch, steps),
        in_specs=[row(SSD_WIDTH), row(CONV_CH), row(LANES),
                  const((CONV_WIDTH, CONV_CH)), const((1, CONV_CH)), const((1, LANES)), const((1, LANES)),
                  const((1, SSD_WIDTH)), const((1, SSD_WIDTH))],
        out_specs=row(SSD_WIDTH),
        out_shape=jax.ShapeDtypeStruct((t, SSD_WIDTH), BF16),
        scratch_shapes=[pltpu.VMEM((SSD_TM + 8, CONV_CH), F32), pltpu.VMEM((SSD_STATE, SSD_WIDTH), F32)],
        compiler_params=_cparams(("parallel", "arbitrary")),
        name="ssd",
    )(z, xbc, dt, conv_w.astype(F32), conv_b.astype(F32)[None, :], pad(dt_bias), pad(a_log),
      jnp.repeat(d_skip.astype(F32), SSD_HEAD_DIM)[None, :], norm_w.astype(F32)[None, :])


def _mem_kernel(qm_ref, mem_ref, wkv_ref, mb_ref, o_ref, k_sc, v_sc):
    @pl.when(pl.program_id(1) == 0)
    def _():
        kv = _dot(mem_ref[0].astype(BF16), wkv_ref[...])
        k_sc[...] = kv[:, :MEM_WIDTH].astype(BF16)
        v_sc[...] = kv[:, MEM_WIDTH:].astype(BF16)

    tq = qm_ref.shape[0]
    head0 = lax.broadcasted_iota(jnp.int32, (tq, LANES), 1) < HEAD_DIM
    outs = []
    for hp in range(MEM_WIDTH // LANES):
        sl = slice(hp * LANES, (hp + 1) * LANES)
        qp = qm_ref[:, sl]
        kp = k_sc[:, sl]
        vp = v_sc[:, sl]
        res = []
        for hh in range(2):
            h = 2 * hp + hh
            keep = head0 if hh == 0 else jnp.logical_not(head0)
            qh = jnp.where(keep, qp, 0.0).astype(BF16)
            s = _dot_nt(qh, kp) + mb_ref[h:h + 1, :]
            m = jnp.max(s, axis=-1, keepdims=True)
            p = jnp.exp(s - m)
            den = jnp.sum(p, axis=-1, keepdims=True)
            res.append(_dot(p.astype(BF16), vp) / den)
        outs.append(jnp.where(head0, res[0], res[1]))
    o_ref[...] = jnp.concatenate(outs, axis=1).astype(o_ref.dtype)


def _memory_attention(qm, mem, wkv, mem_bias, batch, seq):
    t = qm.shape[0]
    steps = seq // MEM_TQ
    mb = jnp.pad(mem_bias.astype(F32), ((0, 8 - N_MEM_HEADS), (0, 0)))
    return pl.pallas_call(
        _mem_kernel,
        grid=(batch, steps),
        in_specs=[pl.BlockSpec((MEM_TQ, MEM_WIDTH), lambda b, j: (b * steps + j, 0)),
                  pl.BlockSpec((1, N_MEM, D_MODEL), lambda b, j: (b, 0, 0)),
                  pl.BlockSpec((D_MODEL, 2 * MEM_WIDTH), lambda b, j: (0, 0)),
                  pl.BlockSpec((8, N_MEM), lambda b, j: (0, 0))],
        out_specs=pl.BlockSpec((MEM_TQ, MEM_WIDTH), lambda b, j: (b * steps + j, 0)),
        out_shape=jax.ShapeDtypeStruct((t, MEM_WIDTH), BF16),
        scratch_shapes=[pltpu.VMEM((N_MEM, MEM_WIDTH), BF16), pltpu.VMEM((N_MEM, MEM_WIDTH), BF16)],
        compiler_params=_cparams(("parallel", "arbitrary")),
        name="mem_attn",
    )(qm, mem, wkv, mb)


def _layer_norm(h, g, b):
    mu = jnp.mean(h, axis=-1, keepdims=True)
    var = jnp.mean(jnp.square(h - mu), axis=-1, keepdims=True)
    return (h - mu) * lax.rsqrt(var + LN_EPS) * g + b


def _outproj_kernel(attn_ref, ssd_ref, memo_ref, x_ref, wo_ref, g_ref, b_ref, wr_ref, br_ref, tri_ref,
                    x1_ref, route_ref, cnt_ref, carry_ref):
    i = pl.program_id(0)

    @pl.when(i == 0)
    def _():
        carry_ref[...] = jnp.zeros_like(carry_ref)

    y = (_dot(attn_ref[...], wo_ref[0:ATTN_WIDTH, :])
         + _dot(ssd_ref[...], wo_ref[ATTN_WIDTH:ATTN_WIDTH + SSD_WIDTH, :])
         + _dot(memo_ref[...], wo_ref[ATTN_WIDTH + SSD_WIDTH:, :]))
    x1 = _layer_norm(ALPHA * _load_rows(x_ref) + y, g_ref[...], b_ref[...])
    _store_rows(x1_ref, x1)

    logits = lax.dot_general(wr_ref[...], x1, (((1,), (1,)), ((), ())), precision=lax.Precision.HIGHEST,
                             preferred_element_type=F32) + br_ref[...]
    mx = jnp.max(logits, axis=0, keepdims=True)
    ex = jnp.exp(logits - mx)
    sc = ex / jnp.sum(ex, axis=0, keepdims=True)
    s = [sc[e:e + 1, :] for e in range(N_EXPERTS)]

    def top2_sum(vals):
        best = None
        for a, b in PAIRS:
            pair = vals[a] + vals[b]
            best = pair if best is None else jnp.maximum(best, pair)
        return best

    grp = [top2_sum(s[4 * g:4 * g + 4]) for g in range(N_EXPERT_GROUPS)]
    best_g = jnp.zeros_like(grp[0], dtype=jnp.int32)
    best_v = grp[0]
    for g in range(1, N_EXPERT_GROUPS):
        upd = grp[g] > best_v
        best_g = jnp.where(upd, g, best_g)
        best_v = jnp.where(upd, grp[g], best_v)
    sel = []
    for jj in range(EXPERTS_PER_GROUP):
        v = s[3 * EXPERTS_PER_GROUP + jj]
        for g in range(N_EXPERT_GROUPS - 2, -1, -1):
            v = jnp.where(best_g == g, s[g * EXPERTS_PER_GROUP + jj], v)
        sel.append(v)

    def argmax_first(vals):
        idx = jnp.zeros_like(best_g)
        val = vals[0]
        for jj in range(1, len(vals)):
            upd = vals[jj] > val
            idx = jnp.where(upd, jj, idx)
            val = jnp.where(upd, vals[jj], val)
        return idx, val

    i1, _ = argmax_first(sel)
    i2, _ = argmax_first([jnp.where(i1 == jj, -1.0, sel[jj]) for jj in range(EXPERTS_PER_GROUP)])
    lo = jnp.minimum(i1, i2)
    hi = jnp.maximum(i1, i2)
    pair_idx = jnp.where(lo == 0, hi - 1, jnp.where(lo == 1, hi + 1, 5))
    cls = best_g * len(PAIRS) + pair_idx

    tm = cls.shape[1]
    onehot = (lax.broadcasted_iota(jnp.int32, (CLS_PAD, tm), 0) == cls)
    prefix = _dot(onehot.astype(BF16), tri_ref[...])
    oh = onehot.astype(F32)
    carry = carry_ref[:, 0:1]
    rank = jnp.sum(oh * (prefix + carry), axis=0, keepdims=True)
    new_carry = carry + jnp.sum(oh, axis=1, keepdims=True)
    carry_ref[...] = jnp.broadcast_to(new_carry, carry_ref.shape)
    cnt_ref[...] = jnp.broadcast_to(new_carry, cnt_ref.shape)

    route_ref[...] = jnp.concatenate([cls.astype(F32), rank, jnp.zeros((6, tm), F32)], axis=0)


def _outproj_route(attn, ssd, memo, x, wo, g, b, wr_t, br_col, tri, t):
    row = lambda n: pl.BlockSpec((OUT_TM, n), lambda i: (i, 0))
    const = lambda shape: pl.BlockSpec(shape, lambda i: (0, 0))
    return pl.pallas_call(
        _outproj_kernel,
        grid=(t // OUT_TM,),
        in_specs=[row(ATTN_WIDTH), row(SSD_WIDTH), row(MEM_WIDTH), _row_spec(OUT_TM, x),
                  const((D_MODEL, D_MODEL)), const((1, D_MODEL)), const((1, D_MODEL)),
                  const((N_EXPERTS, D_MODEL)), const((N_EXPERTS, 1)), const((OUT_TM, OUT_TM))],
        out_specs=[pl.BlockSpec((OUT_TM,) + TOKEN_ROW, lambda i: (i,) + (0,) * len(TOKEN_ROW)),
                   pl.BlockSpec((8, OUT_TM), lambda i: (0, i)), const((CLS_PAD, LANES))],
        out_shape=[jax.ShapeDtypeStruct((t,) + TOKEN_ROW, F32), jax.ShapeDtypeStruct((8, t), F32),
                   jax.ShapeDtypeStruct((CLS_PAD, LANES), F32)],
        scratch_shapes=[pltpu.VMEM((CLS_PAD, LANES), F32)],
        compiler_params=_cparams(("arbitrary",)),
        name="outproj_route",
    )(attn, ssd, memo, x, wo, g, b, wr_t, br_col, tri)


def _moe_kernel(ea_ref, eb_ref, nv_ref, tok_ref,
                x1_hbm, wga, wua, wda, wgb, wub, wdb, g_ref, b_ref, wr_ref, br_ref,
                out_hbm, xbuf0, xbuf1, obuf0, obuf1, gsem, ssem, *, n_tok, first_expert):
    i = pl.program_id(0)
    n_used = nv_ref[pl.num_programs(0)]
    xbufs = (xbuf0, xbuf1)
    obufs = (obuf0, obuf1)

    def gather_copy(tok, r, p):
        src = x1_hbm.at[pl.ds(jnp.minimum(tok, n_tok - 1), 1)]
        return pltpu.make_async_copy(src, xbufs[p].at[pl.ds(r, 1)], gsem.at[p])

    def scatter_copy(dst, r, p):
        return pltpu.make_async_copy(obufs[p].at[pl.ds(r, 1)], out_hbm.at[pl.ds(dst, 1)], ssem.at[p])

    def tile_gather_done(p):
        return pltpu.make_async_copy(x1_hbm.at[pl.ds(0, MOE_TM)], xbufs[p], gsem.at[p])

    def tile_scatter_done(p):
        return pltpu.make_async_copy(obufs[p], out_hbm.at[pl.ds(0, MOE_TM)], ssem.at[p])

    @pl.when(i == 0)
    def _():
        for p in range(2):
            obufs[p][...] = jnp.zeros_like(obufs[p])
            spare = pltpu.make_async_copy(obufs[p], out_hbm.at[pl.ds(n_tok + p * MOE_TM, MOE_TM)], ssem.at[p])
            spare.start()
            spare.wait()

        def first(r, carry):
            gather_copy(tok_ref[r], r, 0).start()
            return carry

        lax.fori_loop(0, MOE_TM, first, 0)

    def step(p):
        q = 1 - p
        tile_gather_done(p).wait()
        next_base = jnp.minimum(i + 1, n_used - 1) * MOE_TM
        prev_base = jnp.maximum(i - 1, 0) * MOE_TM
        is_first = i == 0

        def issue_rows(lo, hi):
            for r in range(lo, hi):
                gather_copy(tok_ref[next_base + r], r, q).start()
                dst = jnp.where(is_first, n_tok + q * MOE_TM + r, tok_ref[prev_base + r])
                scatter_copy(dst, r, q).start()

        x = _load_rows(xbufs[p])
        xb = x.astype(BF16)
        ea = ea_ref[i] - first_expert
        eb = eb_ref[i] - first_expert
        w_diff = wr_ref[pl.ds(eb, 1), :] - wr_ref[pl.ds(ea, 1), :]
        logit_diff = (jnp.sum(x * w_diff, axis=-1, keepdims=True)
                      + (br_ref[pl.ds(eb, 1), :] - br_ref[pl.ds(ea, 1), :]))
        gates = (jax.nn.sigmoid(-logit_diff), jax.nn.sigmoid(logit_diff))
        half = D_EXPERT // 2
        rows_per_chunk = MOE_TM // 4
        y = None
        for e, (wg, wu, wd) in enumerate(((wga, wua, wda), (wgb, wub, wdb))):
            acc = None
            for c in range(2):
                chunk = 2 * e + c
                issue_rows(chunk * rows_per_chunk, (chunk + 1) * rows_per_chunk)
                cs = slice(c * half, (c + 1) * half)
                gate = _dot(xb, wg[0, :, cs])
                up = _dot(xb, wu[0, :, cs])
                hid = (gate * jax.nn.sigmoid(gate) * up).astype(BF16)
                part = _dot(hid, wd[0, cs, :])
                acc = part if acc is None else acc + part
            term = gates[e] * acc
            y = term if y is None else y + term
        out = _layer_norm(ALPHA * x + y, g_ref[...], b_ref[...])
        tile_scatter_done(q).wait()
        _store_rows(obufs[p], out)

        @pl.when(i == n_used - 1)
        def _():
            base = i * MOE_TM

            def last(r, carry):
                scatter_copy(tok_ref[base + r], r, p).start()
                return carry

            lax.fori_loop(0, MOE_TM, last, 0)
            tile_scatter_done(p).wait()
            tile_gather_done(q).wait()

    for p in range(2):
        @pl.when(jnp.logical_and(i < n_used, i % 2 == p))
        def _(p=p):
            step(p)


def _moe(x1, tok_of_slot, tile_ea, tile_eb, tile_nv, wg, wu, wd, g, b, wr_t, br_col, n_tok, first_expert):
    nt = tile_ea.shape[0]
    wspec = lambda which: pl.BlockSpec(
        (1, D_MODEL, D_EXPERT), lambda i, ea, eb, nv, tok: ((ea if which == 0 else eb)[i], 0, 0))
    const = lambda shape: pl.BlockSpec(shape, lambda i, ea, eb, nv, tok: (0, 0))
    tile_buf = pltpu.VMEM((MOE_TM,) + TOKEN_ROW, F32)
    grid_spec = pltpu.PrefetchScalarGridSpec(
        num_scalar_prefetch=4,
        grid=(nt,),
        in_specs=[pl.BlockSpec(memory_space=pl.ANY),
                  wspec(0), wspec(0), wspec(0), wspec(1), wspec(1), wspec(1),
                  const((1, D_MODEL)), const((1, D_MODEL)), const((N_EXPERTS, D_MODEL)), const((N_EXPERTS, 1))],
        out_specs=pl.BlockSpec(memory_space=pl.ANY),
        scratch_shapes=[tile_buf, tile_buf, tile_buf, tile_buf,
                        pltpu.SemaphoreType.DMA((2,)), pltpu.SemaphoreType.DMA((2,))],
    )
    return pl.pallas_call(
        functools.partial(_moe_kernel, n_tok=n_tok, first_expert=first_expert),
        grid_spec=grid_spec,
        out_shape=jax.ShapeDtypeStruct((n_tok + 2 * MOE_TM,) + TOKEN_ROW, F32),
        compiler_params=_cparams(("arbitrary",)),
        name="moe_experts",
    )(tile_ea, tile_eb, tile_nv, tok_of_slot, x1, wg, wu, wd, wg, wu, wd, g, b, wr_t, br_col)


def _routing_tables(route, counts, t):
    n_tiles = t // MOE_TM + N_CLASSES
    cls = route[0].astype(jnp.int32)
    rank = route[1].astype(jnp.int32)
    cnt = counts[:N_CLASSES, 0].astype(jnp.int32)
    tiles_per_cls = (cnt + MOE_TM - 1) // MOE_TM
    tile_end = jnp.cumsum(tiles_per_cls)
    tile_start = tile_end - tiles_per_cls
    dest = tile_start[cls] * MOE_TM + rank
    slot = jnp.arange(n_tiles * MOE_TM, dtype=jnp.int32)
    spare_row = t + ((slot // MOE_TM) % 2) * MOE_TM + slot % MOE_TM
    tok_of_slot = spare_row.at[dest].set(jnp.arange(t, dtype=jnp.int32))
    tile_id = jnp.arange(n_tiles, dtype=jnp.int32)
    n_used = tile_end[-1]
    tile_cls = jnp.minimum(jnp.searchsorted(tile_end, jnp.minimum(tile_id, n_used - 1), side="right"),
                           N_CLASSES - 1).astype(jnp.int32)
    tile_nv = jnp.where(tile_id < n_used,
                        jnp.clip(cnt[tile_cls] - (tile_id - tile_start[tile_cls]) * MOE_TM, 0, MOE_TM), 0)
    pair_lo = jnp.asarray([p[0] for p in PAIRS], jnp.int32)
    pair_hi = jnp.asarray([p[1] for p in PAIRS], jnp.int32)
    grp_base = (tile_cls // len(PAIRS)) * EXPERTS_PER_GROUP
    tile_ea = grp_base + pair_lo[tile_cls % len(PAIRS)]
    tile_eb = grp_base + pair_hi[tile_cls % len(PAIRS)]
    return tok_of_slot, tile_ea, tile_eb, jnp.concatenate([tile_nv, n_used[None]]).astype(jnp.int32)


def _cast_kernel(w_ref, o_ref):
    o_ref[0] = w_ref[0, 0].astype(o_ref.dtype)


def _expert_weights_bf16(w):
    depth, n_exp, kk, nn = w.shape
    return pl.pallas_call(
        _cast_kernel,
        grid=(depth * n_exp,),
        in_specs=[pl.BlockSpec((1, 1, kk, nn), lambda i: (i // n_exp, i % n_exp, 0, 0))],
        out_specs=pl.BlockSpec((1, kk, nn), lambda i: (i, 0, 0)),
        out_shape=jax.ShapeDtypeStruct((depth * n_exp, kk, nn), BF16),
        compiler_params=_cparams(("parallel",)),
        name="expert_weight_cast",
    )(w)


def _rearranged_w_in(w):
    scale = HEAD_DIM ** -0.5
    s = np.cumsum([ATTN_WIDTH, ATTN_WIDTH, ATTN_WIDTH, SSD_WIDTH, CONV_CH, N_SSD_HEADS]).tolist()
    q, k, v, z, xbc, dt, qm = (w[:, a:b] for a, b in zip([0] + s, s + [w.shape[1]]))
    dt = jnp.pad(dt, ((0, 0), (0, LANES - N_SSD_HEADS)))
    return jnp.concatenate([q * scale, k, v, z, xbc, qm * scale, dt], axis=1).astype(BF16)


def kernel(x, mem, w_in, w_out, rel_bias, conv_w, conv_b, dt_bias, a_log, d_skip, ssd_norm_w, w_mem_kv, mem_bias,
           ln1_g, ln1_b, ln2_g, ln2_b, w_router, b_router, w_gate, w_up, w_down):
    batch, seq, _ = x.shape
    t = batch * seq
    assert seq % (16 * ATTN_BLK) == 0 and seq % SSD_TM == 0 and t % MOE_TM == 0
    bias = _attn_bias_tables(rel_bias)
    tri = jnp.asarray(np.triu(np.ones((OUT_TM, OUT_TM), np.float32), k=1), BF16)
    wr_t = w_router.astype(F32).T
    br_col = b_router.astype(F32)[:, None]
    wg, wu, wd = (_expert_weights_bf16(w) for w in (w_gate, w_up, w_down))
    h = x.reshape(t, D_MODEL)
    for l in range(DEPTH):
        q, k, v, z, xbc, qm, dt = _inproj(h, _rearranged_w_in(w_in[l]), t)
        attn = _attention(q, k, v, bias, batch, seq)
        ssd = _ssd(z, xbc, dt, conv_w[l], conv_b[l], dt_bias[l], a_log[l], d_skip[l], ssd_norm_w[l], batch, seq)
        memo = _memory_attention(qm, mem, w_mem_kv[l].astype(BF16), mem_bias[l], batch, seq)
        x1, route, counts = _outproj_route(attn, ssd, memo, h, w_out[l].astype(BF16),
                                           ln1_g[l][None, :], ln1_b[l][None, :], wr_t, br_col, tri, t)
        tok_of_slot, tile_ea, tile_eb, tile_nv = _routing_tables(route, counts, t)
        h = _moe(x1, tok_of_slot, tile_ea + l * N_EXPERTS, tile_eb + l * N_EXPERTS, tile_nv,
                 wg, wu, wd, ln2_g[l][None, :], ln2_b[l][None, :], wr_t, br_col, t, l * N_EXPERTS)
    return h[:t].reshape(batch, seq, D_MODEL)
```

```python
import functools
import math

import numpy as np
import jax
import jax.numpy as jnp
from jax import lax
from jax.experimental import pallas as pl
from jax.experimental.pallas import tpu as pltpu

F32 = jnp.float32
BF16 = jnp.bfloat16

D_MODEL = 1024
HEAD_DIM = 64
N_ATTN_HEADS = 8
ATTN_WIDTH = N_ATTN_HEADS * HEAD_DIM
DILATED_BRANCHES = ((128, 1), (512, 4), (2048, 16))
ATTN_BLK = 128
ATTN_UNROLL = 8
N_SSD_HEADS = 4
SSD_HEAD_DIM = 64
SSD_WIDTH = N_SSD_HEADS * SSD_HEAD_DIM
SSD_GROUPS = 2
SSD_STATE = 128
CONV_WIDTH = 4
SSD_CHUNK = 128
CONV_CH = SSD_WIDTH + 2 * SSD_GROUPS * SSD_STATE
N_MEM_HEADS = 4
MEM_WIDTH = N_MEM_HEADS * HEAD_DIM
N_MEM = 256
N_BUCKETS = 32
MAX_DISTANCE = 2048
N_EXPERTS = 16
N_EXPERT_GROUPS = 4
EXPERTS_PER_GROUP = 4
D_EXPERT = 1024
DEPTH = 2
ALPHA = (2 * DEPTH) ** 0.25
LN_EPS = 1e-5
RMS_EPS = 1e-5

LANES = 128
NEG = -1e30
PAIRS = ((0, 1), (0, 2), (0, 3), (1, 2), (1, 3), (2, 3))
N_CLASSES = N_EXPERT_GROUPS * len(PAIRS)
CLS_PAD = 32

_C_Q, _C_K, _C_V = 0, ATTN_WIDTH, 2 * ATTN_WIDTH
_C_Z = 3 * ATTN_WIDTH
_C_XBC = _C_Z + SSD_WIDTH
_C_QM = _C_XBC + CONV_CH
_C_DT = _C_QM + MEM_WIDTH
IN_COLS = _C_DT + LANES
TOKEN_ROW = (D_MODEL,)

IN_TM = 512
MEM_TQ = 512
SSD_TM = 512
OUT_TM = 512
MOE_TM = 256
VMEM_LIMIT = 56 * 1024 * 1024


def _cparams(sem):
    return pltpu.CompilerParams(dimension_semantics=sem, vmem_limit_bytes=VMEM_LIMIT)


def _dot(a, b):
    return jnp.dot(a, b, preferred_element_type=F32)


def _dot_nt(a, b):
    return lax.dot_general(a, b, (((1,), (1,)), ((), ())), preferred_element_type=F32)


def _load_rows(ref):
    if len(ref.shape) == 2:
        return ref[...]
    return jnp.concatenate([ref[:, k, :] for k in range(ref.shape[1])], axis=1)


def _store_rows(ref, val):
    if len(ref.shape) == 2:
        ref[...] = val
        return
    for k in range(ref.shape[1]):
        ref[:, k, :] = val[:, k * LANES:(k + 1) * LANES]


def _row_spec(tm, x):
    if x.ndim == 2:
        return pl.BlockSpec((tm, x.shape[1]), lambda i: (i, 0))
    return pl.BlockSpec((tm,) + x.shape[1:], lambda i: (i, 0, 0))


def _inproj_kernel(x_ref, w_ref, q_ref, k_ref, v_ref, z_ref, xbc_ref, qm_ref, dt_ref):
    xb = _load_rows(x_ref).astype(BF16)
    for ref, lo, hi in ((q_ref, _C_Q, _C_K), (k_ref, _C_K, _C_V), (v_ref, _C_V, _C_Z), (z_ref, _C_Z, _C_XBC),
                        (xbc_ref, _C_XBC, _C_QM), (qm_ref, _C_QM, _C_DT), (dt_ref, _C_DT, IN_COLS)):
        ref[...] = _dot(xb, w_ref[:, lo:hi]).astype(ref.dtype)


def _inproj(x, w, t):
    widths = (ATTN_WIDTH, ATTN_WIDTH, ATTN_WIDTH, SSD_WIDTH, CONV_CH, MEM_WIDTH, LANES)
    return pl.pallas_call(
        _inproj_kernel,
        grid=(t // IN_TM,),
        in_specs=[_row_spec(IN_TM, x), pl.BlockSpec((D_MODEL, IN_COLS), lambda i: (0, 0))],
        out_specs=[pl.BlockSpec((IN_TM, wd), lambda i: (i, 0)) for wd in widths],
        out_shape=[jax.ShapeDtypeStruct((t, wd), F32) for wd in widths],
        compiler_params=_cparams(("parallel",)),
        name="inproj",
    )(x, w)


def _attn_kernel(q_ref, k_ref, v_ref, bias_ref, o_ref, obuf, mbuf, dbuf, *, seq):
    def attend(bi, d, qs, ks, qrows, bias_of_head):
        head0 = lax.broadcasted_iota(jnp.int32, (qrows, LANES), 1) < HEAD_DIM
        qb = q_ref[0, pl.ds(qs, qrows, stride=d), :]
        kb = k_ref[0, pl.ds(ks, 2 * ATTN_BLK, stride=d), :].astype(BF16)
        vb = v_ref[0, pl.ds(ks, 2 * ATTN_BLK, stride=d), :].astype(BF16)
        outs, maxes, dens = [], [], []
        for h in range(2):
            keep = head0 if h == 0 else jnp.logical_not(head0)
            qh = jnp.where(keep, qb, 0.0).astype(BF16)
            s = _dot_nt(qh, kb) + bias_of_head(h)
            m = jnp.max(s, axis=-1, keepdims=True)
            p = jnp.exp(s - m)
            outs.append(_dot(p.astype(BF16), vb))
            maxes.append(m)
            dens.append(jnp.sum(p, axis=-1, keepdims=True))
        rows = pl.ds(qs, qrows, stride=d)
        obuf[bi, rows, :] = jnp.where(head0, outs[0], outs[1])
        mbuf[bi, rows, :] = jnp.where(head0, maxes[0], maxes[1])
        dbuf[bi, rows, :] = jnp.where(head0, dens[0], dens[1])

    for bi, (_, d) in enumerate(DILATED_BRANCHES):
        nb = seq // d // ATTN_BLK
        if nb == 2:
            def pair(r, carry, bi=bi, d=d):
                attend(bi, d, r, r, 2 * ATTN_BLK,
                       lambda h: jnp.concatenate([bias_ref[bi, 1, h], bias_ref[bi, 0, h]], axis=0))
                return carry

            lax.fori_loop(0, d, pair, 0, unroll=ATTN_UNROLL // 2)
            continue

        def block(idx, carry, bi=bi, d=d, nb=nb):
            r = idx // nb
            n = idx % nb
            first = 1 - jnp.minimum(n, 1)
            attend(bi, d, r + d * ATTN_BLK * n, r + d * ATTN_BLK * jnp.maximum(n - 1, 0), ATTN_BLK,
                   lambda h: bias_ref[bi, first, h])
            return carry

        lax.fori_loop(0, d * nb, block, 0, unroll=ATTN_UNROLL)

    rows = 256

    def merge(i, carry):
        sl = pl.ds(pl.multiple_of(i * rows, rows), rows)
        m0, m1, m2 = mbuf[0, sl, :], mbuf[1, sl, :], mbuf[2, sl, :]
        m = jnp.maximum(jnp.maximum(m0, m1), m2)
        w0, w1, w2 = jnp.exp(m0 - m), jnp.exp(m1 - m), jnp.exp(m2 - m)
        num = w0 * obuf[0, sl, :] + w1 * obuf[1, sl, :] + w2 * obuf[2, sl, :]
        den = w0 * dbuf[0, sl, :] + w1 * dbuf[1, sl, :] + w2 * dbuf[2, sl, :]
        o_ref[0, sl, :] = (num / den).astype(o_ref.dtype)
        return carry

    lax.fori_loop(0, seq // rows, merge, 0)


def _attention(q, k, v, bias, batch, seq):
    q3, k3, v3 = (a.reshape(batch, seq, ATTN_WIDTH) for a in (q, k, v))
    n_pairs = ATTN_WIDTH // LANES
    blk = pl.BlockSpec((1, seq, LANES), lambda b, hp: (b, 0, hp))
    out = pl.pallas_call(
        functools.partial(_attn_kernel, seq=seq),
        grid=(batch, n_pairs),
        in_specs=[blk, blk, blk,
                  pl.BlockSpec((3, 2, 2, ATTN_BLK, 2 * ATTN_BLK), lambda b, hp: (0, 0, hp, 0, 0))],
        out_specs=blk,
        out_shape=jax.ShapeDtypeStruct((batch, seq, ATTN_WIDTH), BF16),
        scratch_shapes=[pltpu.VMEM((3, seq, LANES), F32)] * 3,
        compiler_params=_cparams(("parallel", "parallel")),
        name="dilated_attn",
    )(q3, k3, v3, bias)
    return out.reshape(batch * seq, ATTN_WIDTH)


def _t5_bucket_np(dist):
    n = np.maximum(dist, 0)
    max_exact = N_BUCKETS // 2
    nf = np.maximum(n, max_exact).astype(np.float32)
    large = max_exact + (np.log(nf / np.float32(max_exact)) / np.float32(math.log(MAX_DISTANCE / max_exact))
                         * np.float32(N_BUCKETS - max_exact)).astype(np.int32)
    large = np.minimum(large, N_BUCKETS - 1)
    return np.where(n < max_exact, n, large)


def _attn_bias_tables(rel_bias):
    period = 3 * ATTN_BLK
    steps = np.arange(ATTN_BLK + 1)
    tables = []
    for _, d in DILATED_BRANCHES:
        onehot = np.eye(N_BUCKETS, dtype=np.float32)[_t5_bucket_np(steps * d)]
        tbl = jnp.dot(jnp.asarray(onehot), rel_bias.astype(F32), precision=lax.Precision.HIGHEST)
        masked = lambda n: jnp.full((n, N_ATTN_HEADS), NEG, F32)
        v_full = jnp.concatenate([tbl[::-1], masked(period - ATTN_BLK - 1)], axis=0)
        v_first = jnp.concatenate([tbl[:1], masked(period - ATTN_BLK), tbl[1:ATTN_BLK][::-1]], axis=0)
        variants = []
        for vec in (v_full, v_first):
            flat = jnp.tile(vec, (ATTN_BLK, 1))[:ATTN_BLK * (period - 1)]
            mat = flat.reshape(ATTN_BLK, period - 1, N_ATTN_HEADS)[:, :2 * ATTN_BLK]
            variants.append(mat.transpose(2, 0, 1))
        tables.append(jnp.stack(variants))
    return jnp.stack(tables)


def _per_head_lanes(cols, lane_idx):
    out = cols[:, N_SSD_HEADS - 1:N_SSD_HEADS]
    for h in range(N_SSD_HEADS - 2, -1, -1):
        out = jnp.where(lane_idx < (h + 1) * SSD_HEAD_DIM, cols[:, h:h + 1], out)
    return out


def _ssd_kernel(z_ref, xbc_ref, dt_ref, cw_ref, cb_ref, dtb_ref, a_ref, dsk_ref, nw_ref, o_ref,
                ext_ref, state_ref):
    j = pl.program_id(1)
    tm = xbc_ref.shape[0]
    halo = 8

    @pl.when(j == 0)
    def _():
        ext_ref[0:halo, :] = jnp.zeros((halo, CONV_CH), F32)
        state_ref[...] = jnp.zeros_like(state_ref)

    ext_ref[halo:halo + tm, :] = xbc_ref[...]

    c = SSD_CHUNK
    row_i = lax.broadcasted_iota(jnp.int32, (c, c), 0)
    col_i = lax.broadcasted_iota(jnp.int32, (c, c), 1)
    tril = row_i >= col_i
    tril_f = tril.astype(F32)
    lane256 = lax.broadcasted_iota(jnp.int32, (c, SSD_WIDTH), 1)
    head_even = lax.broadcasted_iota(jnp.int32, (c, LANES), 1) < SSD_HEAD_DIM
    a_row = -jnp.exp(a_ref[...])

    for ci in range(tm // c):
        base = halo + ci * c
        acc = cb_ref[...]
        for kk in range(CONV_WIDTH):
            shift = CONV_WIDTH - 1 - kk
            acc = acc + cw_ref[kk:kk + 1, :] * ext_ref[base - shift:base - shift + c, :]
        xc = acc * jax.nn.sigmoid(acc)
        xs = xc[:, :SSD_WIDTH]
        bm = xc[:, SSD_WIDTH:SSD_WIDTH + SSD_GROUPS * SSD_STATE]
        cm = xc[:, SSD_WIDTH + SSD_GROUPS * SSD_STATE:]

        dt = jax.nn.softplus(dt_ref[ci * c:(ci + 1) * c, :] + dtb_ref[...])
        adt = dt * a_row
        acs = jnp.dot(tril_f, adt, precision=lax.Precision.HIGHEST, preferred_element_type=F32)
        acs_t = acs.T
        a_last = acs[c - 1:c, :]

        dt_b = _per_head_lanes(dt, lane256)
        acs_b = _per_head_lanes(acs, lane256)
        alast_b = _per_head_lanes(a_last, lane256[:1])
        xdt = xs * dt_b
        xdt_bf = xdt.astype(BF16)
        xdec_bf = (xdt * jnp.exp(alast_b - acs_b)).astype(BF16)

        y_pairs, new_states = [], []
        for g in range(SSD_GROUPS):
            sl = slice(g * LANES, (g + 1) * LANES)
            bg = bm[:, sl]
            cg_bf = cm[:, sl].astype(BF16)
            gmat = _dot_nt(cg_bf, bg.astype(BF16))
            ys = []
            for hh in range(2):
                h = 2 * g + hh
                diff = acs[:, h:h + 1] - acs_t[h:h + 1, :]
                lmat = jnp.exp(jnp.where(tril, diff, NEG))
                ys.append(_dot((gmat * lmat).astype(BF16), xdt_bf[:, sl]))
            y_diag = jnp.where(head_even, ys[0], ys[1])
            y_off = _dot(cg_bf, state_ref[:, sl].astype(BF16))
            y_pairs.append((y_diag, y_off))
            new_states.append(_dot(bg.T.astype(BF16), xdec_bf[:, sl]))
        y_diag = jnp.concatenate([y_pairs[0][0], y_pairs[1][0]], axis=1)
        y_off = jnp.concatenate([y_pairs[0][1], y_pairs[1][1]], axis=1)
        state_ref[...] = state_ref[...] * jnp.exp(alast_b) + jnp.concatenate(new_states, axis=1)

        y = y_diag + y_off * jnp.exp(acs_b) + dsk_ref[...] * xs
        zc = z_ref[ci * c:(ci + 1) * c, :]
        y = y * (zc * jax.nn.sigmoid(zc))
        y = y * lax.rsqrt(jnp.mean(jnp.square(y), axis=-1, keepdims=True) + RMS_EPS) * nw_ref[...]
        o_ref[ci * c:(ci + 1) * c, :] = y.astype(o_ref.dtype)

    ext_ref[0:halo, :] = ext_ref[tm:tm + halo, :]


def _ssd(z, xbc, dt, conv_w, conv_b, dt_bias, a_log, d_skip, norm_w, batch, seq):
    t = z.shape[0]
    steps = seq // SSD_TM
    pad = lambda a: jnp.pad(a.astype(F32), (0, LANES - a.shape[0]))[None, :]
    row = lambda n: pl.BlockSpec((SSD_TM, n), lambda b, j: (b * steps + j, 0))
    const = lambda shape: pl.BlockSpec(shape, lambda b, j: (0, 0))
    return pl.pallas_call(
        _ssd_kernel,
        grid=(batch, steps),
        in_specs=[row(SSD_WIDTH), row(CONV_CH), row(LANES),
                  const((CONV_WIDTH, CONV_CH)), const((1, CONV_CH)), const((1, LANES)), const((1, LANES)),
                  const((1, SSD_WIDTH)), const((1, SSD_WIDTH))],
        out_specs=row(SSD_WIDTH),
        out_shape=jax.ShapeDtypeStruct((t, SSD_WIDTH), BF16),
        scratch_shapes=[pltpu.VMEM((SSD_TM + 8, CONV_CH), F32), pltpu.VMEM((SSD_STATE, SSD_WIDTH), F32)],
        compiler_params=_cparams(("parallel", "arbitrary")),
        name="ssd",
    )(z, xbc, dt, conv_w.astype(F32), conv_b.astype(F32)[None, :], pad(dt_bias), pad(a_log),
      jnp.repeat(d_skip.astype(F32), SSD_HEAD_DIM)[None, :], norm_w.astype(F32)[None, :])


def _mem_kernel(qm_ref, mem_ref, wkv_ref, mb_ref, o_ref, k_sc, v_sc):
    @pl.when(pl.program_id(1) == 0)
    def _():
        kv = _dot(mem_ref[0].astype(BF16), wkv_ref[...])
        k_sc[...] = kv[:, :MEM_WIDTH].astype(BF16)
        v_sc[...] = kv[:, MEM_WIDTH:].astype(BF16)

    tq = qm_ref.shape[0]
    head0 = lax.broadcasted_iota(jnp.int32, (tq, LANES), 1) < HEAD_DIM
    outs = []
    for hp in range(MEM_WIDTH // LANES):
        sl = slice(hp * LANES, (hp + 1) * LANES)
        qp = qm_ref[:, sl]
        kp = k_sc[:, sl]
        vp = v_sc[:, sl]
        res = []
        for hh in range(2):
            h = 2 * hp + hh
            keep = head0 if hh == 0 else jnp.logical_not(head0)
            qh = jnp.where(keep, qp, 0.0).astype(BF16)
            s = _dot_nt(qh, kp) + mb_ref[h:h + 1, :]
            m = jnp.max(s, axis=-1, keepdims=True)
            p = jnp.exp(s - m)
            den = jnp.sum(p, axis=-1, keepdims=True)
            res.append(_dot(p.astype(BF16), vp) / den)
        outs.append(jnp.where(head0, res[0], res[1]))
    o_ref[...] = jnp.concatenate(outs, axis=1).astype(o_ref.dtype)


def _memory_attention(qm, mem, wkv, mem_bias, batch, seq):
    t = qm.shape[0]
    steps = seq // MEM_TQ
    mb = jnp.pad(mem_bias.astype(F32), ((0, 8 - N_MEM_HEADS), (0, 0)))
    return pl.pallas_call(
        _mem_kernel,
        grid=(batch, steps),
        in_specs=[pl.BlockSpec((MEM_TQ, MEM_WIDTH), lambda b, j: (b * steps + j, 0)),
                  pl.BlockSpec((1, N_MEM, D_MODEL), lambda b, j: (b, 0, 0)),
                  pl.BlockSpec((D_MODEL, 2 * MEM_WIDTH), lambda b, j: (0, 0)),
                  pl.BlockSpec((8, N_MEM), lambda b, j: (0, 0))],
        out_specs=pl.BlockSpec((MEM_TQ, MEM_WIDTH), lambda b, j: (b * steps + j, 0)),
        out_shape=jax.ShapeDtypeStruct((t, MEM_WIDTH), BF16),
        scratch_shapes=[pltpu.VMEM((N_MEM, MEM_WIDTH), BF16), pltpu.VMEM((N_MEM, MEM_WIDTH), BF16)],
        compiler_params=_cparams(("parallel", "arbitrary")),
        name="mem_attn",
    )(qm, mem, wkv, mb)


def _layer_norm(h, g, b):
    mu = jnp.mean(h, axis=-1, keepdims=True)
    var = jnp.mean(jnp.square(h - mu), axis=-1, keepdims=True)
    return (h - mu) * lax.rsqrt(var + LN_EPS) * g + b


def _outproj_kernel(attn_ref, ssd_ref, memo_ref, x_ref, wo_ref, g_ref, b_ref, wr_ref, br_ref, tri_ref,
                    x1_ref, route_ref, cnt_ref, carry_ref):
    i = pl.program_id(0)

    @pl.when(i == 0)
    def _():
        carry_ref[...] = jnp.zeros_like(carry_ref)

    y = (_dot(attn_ref[...], wo_ref[0:ATTN_WIDTH, :])
         + _dot(ssd_ref[...], wo_ref[ATTN_WIDTH:ATTN_WIDTH + SSD_WIDTH, :])
         + _dot(memo_ref[...], wo_ref[ATTN_WIDTH + SSD_WIDTH:, :]))
    x1 = _layer_norm(ALPHA * _load_rows(x_ref) + y, g_ref[...], b_ref[...])
    _store_rows(x1_ref, x1)

    logits = lax.dot_general(wr_ref[...], x1, (((1,), (1,)), ((), ())), precision=lax.Precision.HIGHEST,
                             preferred_element_type=F32) + br_ref[...]
    mx = jnp.max(logits, axis=0, keepdims=True)
    ex = jnp.exp(logits - mx)
    sc = ex / jnp.sum(ex, axis=0, keepdims=True)
    s = [sc[e:e + 1, :] for e in range(N_EXPERTS)]

    def top2_sum(vals):
        best = None
        for a, b in PAIRS:
            pair = vals[a] + vals[b]
            best = pair if best is None else jnp.maximum(best, pair)
        return best

    grp = [top2_sum(s[4 * g:4 * g + 4]) for g in range(N_EXPERT_GROUPS)]
    best_g = jnp.zeros_like(grp[0], dtype=jnp.int32)
    best_v = grp[0]
    for g in range(1, N_EXPERT_GROUPS):
        upd = grp[g] > best_v
        best_g = jnp.where(upd, g, best_g)
        best_v = jnp.where(upd, grp[g], best_v)
    sel = []
    for jj in range(EXPERTS_PER_GROUP):
        v = s[3 * EXPERTS_PER_GROUP + jj]
        for g in range(N_EXPERT_GROUPS - 2, -1, -1):
            v = jnp.where(best_g == g, s[g * EXPERTS_PER_GROUP + jj], v)
        sel.append(v)

    def argmax_first(vals):
        idx = jnp.zeros_like(best_g)
        val = vals[0]
        for jj in range(1, len(vals)):
            upd = vals[jj] > val
            idx = jnp.where(upd, jj, idx)
            val = jnp.where(upd, vals[jj], val)
        return idx, val

    i1, _ = argmax_first(sel)
    i2, _ = argmax_first([jnp.where(i1 == jj, -1.0, sel[jj]) for jj in range(EXPERTS_PER_GROUP)])
    lo = jnp.minimum(i1, i2)
    hi = jnp.maximum(i1, i2)
    pair_idx = jnp.where(lo == 0, hi - 1, jnp.where(lo == 1, hi + 1, 5))
    cls = best_g * len(PAIRS) + pair_idx

    tm = cls.shape[1]
    onehot = (lax.broadcasted_iota(jnp.int32, (CLS_PAD, tm), 0) == cls)
    prefix = _dot(onehot.astype(BF16), tri_ref[...])
    oh = onehot.astype(F32)
    carry = carry_ref[:, 0:1]
    rank = jnp.sum(oh * (prefix + carry), axis=0, keepdims=True)
    new_carry = carry + jnp.sum(oh, axis=1, keepdims=True)
    carry_ref[...] = jnp.broadcast_to(new_carry, carry_ref.shape)
    cnt_ref[...] = jnp.broadcast_to(new_carry, cnt_ref.shape)

    route_ref[...] = jnp.concatenate([cls.astype(F32), rank, jnp.zeros((6, tm), F32)], axis=0)


def _outproj_route(attn, ssd, memo, x, wo, g, b, wr_t, br_col, tri, t):
    row = lambda n: pl.BlockSpec((OUT_TM, n), lambda i: (i, 0))
    const = lambda shape: pl.BlockSpec(shape, lambda i: (0, 0))
    return pl.pallas_call(
        _outproj_kernel,
        grid=(t // OUT_TM,),
        in_specs=[row(ATTN_WIDTH), row(SSD_WIDTH), row(MEM_WIDTH), _row_spec(OUT_TM, x),
                  const((D_MODEL, D_MODEL)), const((1, D_MODEL)), const((1, D_MODEL)),
                  const((N_EXPERTS, D_MODEL)), const((N_EXPERTS, 1)), const((OUT_TM, OUT_TM))],
        out_specs=[pl.BlockSpec((OUT_TM,) + TOKEN_ROW, lambda i: (i,) + (0,) * len(TOKEN_ROW)),
                   pl.BlockSpec((8, OUT_TM), lambda i: (0, i)), const((CLS_PAD, LANES))],
        out_shape=[jax.ShapeDtypeStruct((t,) + TOKEN_ROW, F32), jax.ShapeDtypeStruct((8, t), F32),
                   jax.ShapeDtypeStruct((CLS_PAD, LANES), F32)],
        scratch_shapes=[pltpu.VMEM((CLS_PAD, LANES), F32)],
        compiler_params=_cparams(("arbitrary",)),
        name="outproj_route",
    )(attn, ssd, memo, x, wo, g, b, wr_t, br_col, tri)


def _moe_kernel(ea_ref, eb_ref, nv_ref, tok_ref,
                x1_hbm, wga, wua, wda, wgb, wub, wdb, g_ref, b_ref, wr_ref, br_ref,
                out_hbm, xbuf0, xbuf1, obuf0, obuf1, gsem, ssem, *, n_tok, first_expert):
    i = pl.program_id(0)
    n_used = nv_ref[pl.num_programs(0)]
    xbufs = (xbuf0, xbuf1)
    obufs = (obuf0, obuf1)

    def gather_copy(tok, r, p):
        src = x1_hbm.at[pl.ds(jnp.minimum(tok, n_tok - 1), 1)]
        return pltpu.make_async_copy(src, xbufs[p].at[pl.ds(r, 1)], gsem.at[p])

    def scatter_copy(dst, r, p):
        return pltpu.make_async_copy(obufs[p].at[pl.ds(r, 1)], out_hbm.at[pl.ds(dst, 1)], ssem.at[p])

    def tile_gather_done(p):
        return pltpu.make_async_copy(x1_hbm.at[pl.ds(0, MOE_TM)], xbufs[p], gsem.at[p])

    def tile_scatter_done(p):
        return pltpu.make_async_copy(obufs[p], out_hbm.at[pl.ds(0, MOE_TM)], ssem.at[p])

    @pl.when(i == 0)
    def _():
        for p in range(2):
            obufs[p][...] = jnp.zeros_like(obufs[p])
            spare = pltpu.make_async_copy(obufs[p], out_hbm.at[pl.ds(n_tok + p * MOE_TM, MOE_TM)], ssem.at[p])
            spare.start()
            spare.wait()

        def first(r, carry):
            gather_copy(tok_ref[r], r, 0).start()
            return carry

        lax.fori_loop(0, MOE_TM, first, 0)

    def step(p):
        q = 1 - p
        tile_gather_done(p).wait()
        next_base = jnp.minimum(i + 1, n_used - 1) * MOE_TM
        prev_base = jnp.maximum(i - 1, 0) * MOE_TM
        is_first = i == 0

        def issue_rows(lo, hi):
            for r in range(lo, hi):
                gather_copy(tok_ref[next_base + r], r, q).start()
                dst = jnp.where(is_first, n_tok + q * MOE_TM + r, tok_ref[prev_base + r])
                scatter_copy(dst, r, q).start()

        @pl.when(n_used > 0)
        def _():
            issue_rows(0, MOE_TM)

        x = _load_rows(xbufs[p])
        xb = x.astype(BF16)
        ea = ea_ref[i] - first_expert
        eb = eb_ref[i] - first_expert
        w_diff = wr_ref[pl.ds(eb, 1), :] - wr_ref[pl.ds(ea, 1), :]
        logit_diff = (jnp.sum(x * w_diff, axis=-1, keepdims=True)
                      + (br_ref[pl.ds(eb, 1), :] - br_ref[pl.ds(ea, 1), :]))
        gates = (jax.nn.sigmoid(-logit_diff), jax.nn.sigmoid(logit_diff))
        half = D_EXPERT // 2
        y = None
        for e, (wg, wu, wd) in enumerate(((wga, wua, wda), (wgb, wub, wdb))):
            acc = None
            for c in range(2):
                cs = slice(c * half, (c + 1) * half)
                gate = _dot(xb, wg[0, :, cs])
                up = _dot(xb, wu[0, :, cs])
                hid = (gate * jax.nn.sigmoid(gate) * up).astype(BF16)
                part = _dot(hid, wd[0, cs, :])
                acc = part if acc is None else acc + part
            term = gates[e] * acc
            y = term if y is None else y + term
        out = _layer_norm(ALPHA * x + y, g_ref[...], b_ref[...])
        tile_scatter_done(q).wait()
        _store_rows(obufs[p], out)

        @pl.when(i == n_used - 1)
        def _():
            base = i * MOE_TM

            def last(r, carry):
                scatter_copy(tok_ref[base + r], r, p).start()
                return carry

            lax.fori_loop(0, MOE_TM, last, 0)
            tile_scatter_done(p).wait()
            tile_gather_done(q).wait()

    for p in range(2):
        @pl.when(jnp.logical_and(i < n_used, i % 2 == p))
        def _(p=p):
            step(p)


def _moe(x1, tok_of_slot, tile_ea, tile_eb, tile_nv, wg, wu, wd, g, b, wr_t, br_col, n_tok, first_expert):
    nt = tile_ea.shape[0]
    wspec = lambda which: pl.BlockSpec(
        (1, D_MODEL, D_EXPERT), lambda i, ea, eb, nv, tok: ((ea if which == 0 else eb)[i], 0, 0))
    const = lambda shape: pl.BlockSpec(shape, lambda i, ea, eb, nv, tok: (0, 0))
    tile_buf = pltpu.VMEM((MOE_TM,) + TOKEN_ROW, F32)
    grid_spec = pltpu.PrefetchScalarGridSpec(
        num_scalar_prefetch=4,
        grid=(nt,),
        in_specs=[pl.BlockSpec(memory_space=pl.ANY),
                  wspec(0), wspec(0), wspec(0), wspec(1), wspec(1), wspec(1),
                  const((1, D_MODEL)), const((1, D_MODEL)), const((N_EXPERTS, D_MODEL)), const((N_EXPERTS, 1))],
        out_specs=pl.BlockSpec(memory_space=pl.ANY),
        scratch_shapes=[tile_buf, tile_buf, tile_buf, tile_buf,
                        pltpu.SemaphoreType.DMA((2,)), pltpu.SemaphoreType.DMA((2,))],
    )
    return pl.pallas_call(
        functools.partial(_moe_kernel, n_tok=n_tok, first_expert=first_expert),
        grid_spec=grid_spec,
        out_shape=jax.ShapeDtypeStruct((n_tok + 2 * MOE_TM,) + TOKEN_ROW, F32),
        compiler_params=_cparams(("arbitrary",)),
        name="moe_experts",
    )(tile_ea, tile_eb, tile_nv, tok_of_slot, x1, wg, wu, wd, wg, wu, wd, g, b, wr_t, br_col)


def _routing_tables(route, counts, t):
    n_tiles = t // MOE_TM + N_CLASSES
    cls = route[0].astype(jnp.int32)
    rank = route[1].astype(jnp.int32)
    cnt = counts[:N_CLASSES, 0].astype(jnp.int32)
    tiles_per_cls = (cnt + MOE_TM - 1) // MOE_TM
    tile_end = jnp.cumsum(tiles_per_cls)
    tile_start = tile_end - tiles_per_cls
    dest = tile_start[cls] * MOE_TM + rank
    slot = jnp.arange(n_tiles * MOE_TM, dtype=jnp.int32)
    spare_row = t + ((slot // MOE_TM) % 2) * MOE_TM + slot % MOE_TM
    tok_of_slot = spare_row.at[dest].set(jnp.arange(t, dtype=jnp.int32))
    tile_id = jnp.arange(n_tiles, dtype=jnp.int32)
    n_used = tile_end[-1]
    tile_cls = jnp.minimum(jnp.searchsorted(tile_end, jnp.minimum(tile_id, n_used - 1), side="right"),
                           N_CLASSES - 1).astype(jnp.int32)
    tile_nv = jnp.where(tile_id < n_used,
                        jnp.clip(cnt[tile_cls] - (tile_id - tile_start[tile_cls]) * MOE_TM, 0, MOE_TM), 0)
    pair_lo = jnp.asarray([p[0] for p in PAIRS], jnp.int32)
    pair_hi = jnp.asarray([p[1] for p in PAIRS], jnp.int32)
    grp_base = (tile_cls // len(PAIRS)) * EXPERTS_PER_GROUP
    tile_ea = grp_base + pair_lo[tile_cls % len(PAIRS)]
    tile_eb = grp_base + pair_hi[tile_cls % len(PAIRS)]
    return tok_of_slot, tile_ea, tile_eb, jnp.concatenate([tile_nv, n_used[None]]).astype(jnp.int32)


def _cast_kernel(w_ref, o_ref):
    o_ref[0] = w_ref[0, 0].astype(o_ref.dtype)


def _expert_weights_bf16(w):
    depth, n_exp, kk, nn = w.shape
    return pl.pallas_call(
        _cast_kernel,
        grid=(depth * n_exp,),
        in_specs=[pl.BlockSpec((1, 1, kk, nn), lambda i: (i // n_exp, i % n_exp, 0, 0))],
        out_specs=pl.BlockSpec((1, kk, nn), lambda i: (i, 0, 0)),
        out_shape=jax.ShapeDtypeStruct((depth * n_exp, kk, nn), BF16),
        compiler_params=_cparams(("parallel",)),
        name="expert_weight_cast",
    )(w)


def _rearranged_w_in(w):
    scale = HEAD_DIM ** -0.5
    s = np.cumsum([ATTN_WIDTH, ATTN_WIDTH, ATTN_WIDTH, SSD_WIDTH, CONV_CH, N_SSD_HEADS]).tolist()
    q, k, v, z, xbc, dt, qm = (w[:, a:b] for a, b in zip([0] + s, s + [w.shape[1]]))
    dt = jnp.pad(dt, ((0, 0), (0, LANES - N_SSD_HEADS)))
    return jnp.concatenate([q * scale, k, v, z, xbc, qm * scale, dt], axis=1).astype(BF16)


def kernel(x, mem, w_in, w_out, rel_bias, conv_w, conv_b, dt_bias, a_log, d_skip, ssd_norm_w, w_mem_kv, mem_bias,
           ln1_g, ln1_b, ln2_g, ln2_b, w_router, b_router, w_gate, w_up, w_down):
    batch, seq, _ = x.shape
    t = batch * seq
    assert seq % (16 * ATTN_BLK) == 0 and seq % SSD_TM == 0 and t % MOE_TM == 0
    bias = _attn_bias_tables(rel_bias)
    tri = jnp.asarray(np.triu(np.ones((OUT_TM, OUT_TM), np.float32), k=1), BF16)
    wr_t = w_router.astype(F32).T
    br_col = b_router.astype(F32)[:, None]
    wg, wu, wd = (_expert_weights_bf16(w) for w in (w_gate, w_up, w_down))
    h = x.reshape(t, D_MODEL)
    for l in range(DEPTH):
        q, k, v, z, xbc, qm, dt = _inproj(h, _rearranged_w_in(w_in[l]), t)
        attn = _attention(q, k, v, bias, batch, seq)
        ssd = _ssd(z, xbc, dt, conv_w[l], conv_b[l], dt_bias[l], a_log[l], d_skip[l], ssd_norm_w[l], batch, seq)
        memo = _memory_attention(qm, mem, w_mem_kv[l].astype(BF16), mem_bias[l], batch, seq)
        x1, route, counts = _outproj_route(attn, ssd, memo, h, w_out[l].astype(BF16),
                                           ln1_g[l][None, :], ln1_b[l][None, :], wr_t, br_col, tri, t)
        tok_of_slot, tile_ea, tile_eb, tile_nv = _routing_tables(route, counts, t)
        h = _moe(x1, tok_of_slot, tile_ea + l * N_EXPERTS, tile_eb + l * N_EXPERTS, tile_nv,
                 wg, wu, wd, ln2_g[l][None, :], ln2_b[l][None, :], wr_t, br_col, t, l * N_EXPERTS)
    return h[:t].reshape(batch, seq, D_MODEL)
```

```python
import functools
import math

import numpy as np
import jax
import jax.numpy as jnp
from jax import lax
from jax.experimental import pallas as pl
from jax.experimental.pallas import tpu as pltpu

F32 = jnp.float32
BF16 = jnp.bfloat16

D_MODEL = 1024
HEAD_DIM = 64
N_ATTN_HEADS = 8
ATTN_WIDTH = N_ATTN_HEADS * HEAD_DIM
DILATED_BRANCHES = ((128, 1), (512, 4), (2048, 16))
ATTN_BLK = 128
ATTN_UNROLL = 8
N_SSD_HEADS = 4
SSD_HEAD_DIM = 64
SSD_WIDTH = N_SSD_HEADS * SSD_HEAD_DIM
SSD_GROUPS = 2
SSD_STATE = 128
CONV_WIDTH = 4
SSD_CHUNK = 128
CONV_CH = SSD_WIDTH + 2 * SSD_GROUPS * SSD_STATE
N_MEM_HEADS = 4
MEM_WIDTH = N_MEM_HEADS * HEAD_DIM
N_MEM = 256
N_BUCKETS = 32
MAX_DISTANCE = 2048
N_EXPERTS = 16
N_EXPERT_GROUPS = 4
EXPERTS_PER_GROUP = 4
D_EXPERT = 1024
DEPTH = 2
ALPHA = (2 * DEPTH) ** 0.25
LN_EPS = 1e-5
RMS_EPS = 1e-5

LANES = 128
NEG = -1e30
PAIRS = ((0, 1), (0, 2), (0, 3), (1, 2), (1, 3), (2, 3))
N_CLASSES = N_EXPERT_GROUPS * len(PAIRS)
CLS_PAD = 32

_C_Q, _C_K, _C_V = 0, ATTN_WIDTH, 2 * ATTN_WIDTH
_C_Z = 3 * ATTN_WIDTH
_C_XBC = _C_Z + SSD_WIDTH
_C_QM = _C_XBC + CONV_CH
_C_DT = _C_QM + MEM_WIDTH
IN_COLS = _C_DT + LANES
TOKEN_ROW = (D_MODEL,)

IN_TM = 512
MEM_TQ = 512
SSD_TM = 512
OUT_TM = 512
MOE_TM = 256
VMEM_LIMIT = 56 * 1024 * 1024


def _cparams(sem):
    return pltpu.CompilerParams(dimension_semantics=sem, vmem_limit_bytes=VMEM_LIMIT)


def _dot(a, b):
    return jnp.dot(a, b, preferred_element_type=F32)


def _dot_nt(a, b):
    return lax.dot_general(a, b, (((1,), (1,)), ((), ())), preferred_element_type=F32)


def _load_rows(ref):
    if len(ref.shape) == 2:
        return ref[...]
    return jnp.concatenate([ref[:, k, :] for k in range(ref.shape[1])], axis=1)


def _store_rows(ref, val):
    if len(ref.shape) == 2:
        ref[...] = val
        return
    for k in range(ref.shape[1]):
        ref[:, k, :] = val[:, k * LANES:(k + 1) * LANES]


def _row_spec(tm, x):
    if x.ndim == 2:
        return pl.BlockSpec((tm, x.shape[1]), lambda i: (i, 0))
    return pl.BlockSpec((tm,) + x.shape[1:], lambda i: (i, 0, 0))


def _cast_slab_specs(w, layer, steps):
    _, n_exp, kk, nn = w.shape
    if steps >= n_exp:
        per = steps // n_exp
        rows = kk // per
        in_spec = pl.BlockSpec((1, 1, rows, nn), lambda i: (layer, i // per, i % per, 0))
        out_spec = pl.BlockSpec((1, rows, nn), lambda i: (i // per, i % per, 0))
    else:
        per = n_exp // steps
        in_spec = pl.BlockSpec((1, per, kk, nn), lambda i: (layer, i, 0, 0))
        out_spec = pl.BlockSpec((per, kk, nn), lambda i: (i, 0, 0))
    return in_spec, out_spec, jax.ShapeDtypeStruct((n_exp, kk, nn), BF16)


def _cast_slabs(refs):
    n = len(refs) // 2
    for w_ref, o_ref in zip(refs[:n], refs[n:]):
        o_ref[...] = w_ref[0].astype(o_ref.dtype)


def _inproj_kernel(x_ref, w_ref, cast_in, q_ref, k_ref, v_ref, z_ref, xbc_ref, qm_ref, dt_ref, cast_out):
    xb = _load_rows(x_ref).astype(BF16)
    for ref, lo, hi in ((q_ref, _C_Q, _C_K), (k_ref, _C_K, _C_V), (v_ref, _C_V, _C_Z), (z_ref, _C_Z, _C_XBC),
                        (xbc_ref, _C_XBC, _C_QM), (qm_ref, _C_QM, _C_DT), (dt_ref, _C_DT, IN_COLS)):
        ref[...] = _dot(xb, w_ref[:, lo:hi]).astype(ref.dtype)
    _cast_slabs((cast_in, cast_out))


def _inproj(x, w, t, expert_w, layer):
    widths = (ATTN_WIDTH, ATTN_WIDTH, ATTN_WIDTH, SSD_WIDTH, CONV_CH, MEM_WIDTH, LANES)
    steps = t // IN_TM
    cast_in, cast_out, cast_shape = _cast_slab_specs(expert_w, layer, steps)
    return pl.pallas_call(
        _inproj_kernel,
        grid=(steps,),
        in_specs=[_row_spec(IN_TM, x), pl.BlockSpec((D_MODEL, IN_COLS), lambda i: (0, 0)), cast_in],
        out_specs=[pl.BlockSpec((IN_TM, wd), lambda i: (i, 0)) for wd in widths] + [cast_out],
        out_shape=[jax.ShapeDtypeStruct((t, wd), F32) for wd in widths] + [cast_shape],
        compiler_params=_cparams(("parallel",)),
        name="inproj",
    )(x, w, expert_w)


def _attn_kernel(q_ref, k_ref, v_ref, bias_ref, o_ref, obuf, mbuf, dbuf, *, seq):
    def attend(bi, d, qs, ks, qrows, bias_of_head):
        head0 = lax.broadcasted_iota(jnp.int32, (qrows, LANES), 1) < HEAD_DIM
        qb = q_ref[0, pl.ds(qs, qrows, stride=d), :]
        kb = k_ref[0, pl.ds(ks, 2 * ATTN_BLK, stride=d), :].astype(BF16)
        vb = v_ref[0, pl.ds(ks, 2 * ATTN_BLK, stride=d), :].astype(BF16)
        outs, maxes, dens = [], [], []
        for h in range(2):
            keep = head0 if h == 0 else jnp.logical_not(head0)
            qh = jnp.where(keep, qb, 0.0).astype(BF16)
            s = _dot_nt(qh, kb) + bias_of_head(h)
            m = jnp.max(s, axis=-1, keepdims=True)
            p = jnp.exp(s - m)
            outs.append(_dot(p.astype(BF16), vb))
            maxes.append(m)
            dens.append(jnp.sum(p, axis=-1, keepdims=True))
        rows = pl.ds(qs, qrows, stride=d)
        obuf[bi, rows, :] = jnp.where(head0, outs[0], outs[1])
        mbuf[bi, rows, :] = jnp.where(head0, maxes[0], maxes[1])
        dbuf[bi, rows, :] = jnp.where(head0, dens[0], dens[1])

    for bi, (_, d) in enumerate(DILATED_BRANCHES):
        nb = seq // d // ATTN_BLK
        if nb == 2:
            def pair(r, carry, bi=bi, d=d):
                attend(bi, d, r, r, 2 * ATTN_BLK,
                       lambda h: jnp.concatenate([bias_ref[bi, 1, h], bias_ref[bi, 0, h]], axis=0))
                return carry

            lax.fori_loop(0, d, pair, 0, unroll=ATTN_UNROLL // 2)
            continue

        def block(idx, carry, bi=bi, d=d, nb=nb):
            r = idx // nb
            n = idx % nb
            first = 1 - jnp.minimum(n, 1)
            attend(bi, d, r + d * ATTN_BLK * n, r + d * ATTN_BLK * jnp.maximum(n - 1, 0), ATTN_BLK,
                   lambda h: bias_ref[bi, first, h])
            return carry

        lax.fori_loop(0, d * nb, block, 0, unroll=ATTN_UNROLL)

    rows = 256

    def merge(i, carry):
        sl = pl.ds(pl.multiple_of(i * rows, rows), rows)
        m0, m1, m2 = mbuf[0, sl, :], mbuf[1, sl, :], mbuf[2, sl, :]
        m = jnp.maximum(jnp.maximum(m0, m1), m2)
        w0, w1, w2 = jnp.exp(m0 - m), jnp.exp(m1 - m), jnp.exp(m2 - m)
        num = w0 * obuf[0, sl, :] + w1 * obuf[1, sl, :] + w2 * obuf[2, sl, :]
        den = w0 * dbuf[0, sl, :] + w1 * dbuf[1, sl, :] + w2 * dbuf[2, sl, :]
        o_ref[0, sl, :] = (num / den).astype(o_ref.dtype)
        return carry

    lax.fori_loop(0, seq // rows, merge, 0)


def _attention(q, k, v, bias, batch, seq):
    q3, k3, v3 = (a.reshape(batch, seq, ATTN_WIDTH) for a in (q, k, v))
    n_pairs = ATTN_WIDTH // LANES
    blk = pl.BlockSpec((1, seq, LANES), lambda b, hp: (b, 0, hp))
    out = pl.pallas_call(
        functools.partial(_attn_kernel, seq=seq),
        grid=(batch, n_pairs),
        in_specs=[blk, blk, blk,
                  pl.BlockSpec((3, 2, 2, ATTN_BLK, 2 * ATTN_BLK), lambda b, hp: (0, 0, hp, 0, 0))],
        out_specs=blk,
        out_shape=jax.ShapeDtypeStruct((batch, seq, ATTN_WIDTH), BF16),
        scratch_shapes=[pltpu.VMEM((3, seq, LANES), F32)] * 3,
        compiler_params=_cparams(("parallel", "parallel")),
        name="dilated_attn",
    )(q3, k3, v3, bias)
    return out.reshape(batch * seq, ATTN_WIDTH)


def _t5_bucket_np(dist):
    n = np.maximum(dist, 0)
    max_exact = N_BUCKETS // 2
    nf = np.maximum(n, max_exact).astype(np.float32)
    large = max_exact + (np.log(nf / np.float32(max_exact)) / np.float32(math.log(MAX_DISTANCE / max_exact))
                         * np.float32(N_BUCKETS - max_exact)).astype(np.int32)
    large = np.minimum(large, N_BUCKETS - 1)
    return np.where(n < max_exact, n, large)


def _attn_bias_tables(rel_bias):
    period = 3 * ATTN_BLK
    steps = np.arange(ATTN_BLK + 1)
    tables = []
    for _, d in DILATED_BRANCHES:
        onehot = np.eye(N_BUCKETS, dtype=np.float32)[_t5_bucket_np(steps * d)]
        tbl = jnp.dot(jnp.asarray(onehot), rel_bias.astype(F32), precision=lax.Precision.HIGHEST)
        masked = lambda n: jnp.full((n, N_ATTN_HEADS), NEG, F32)
        v_full = jnp.concatenate([tbl[::-1], masked(period - ATTN_BLK - 1)], axis=0)
        v_first = jnp.concatenate([tbl[:1], masked(period - ATTN_BLK), tbl[1:ATTN_BLK][::-1]], axis=0)
        variants = []
        for vec in (v_full, v_first):
            flat = jnp.tile(vec, (ATTN_BLK, 1))[:ATTN_BLK * (period - 1)]
            mat = flat.reshape(ATTN_BLK, period - 1, N_ATTN_HEADS)[:, :2 * ATTN_BLK]
            variants.append(mat.transpose(2, 0, 1))
        tables.append(jnp.stack(variants))
    return jnp.stack(tables)


def _per_head_lanes(cols, lane_idx):
    out = cols[:, N_SSD_HEADS - 1:N_SSD_HEADS]
    for h in range(N_SSD_HEADS - 2, -1, -1):
        out = jnp.where(lane_idx < (h + 1) * SSD_HEAD_DIM, cols[:, h:h + 1], out)
    return out


def _ssd_kernel(z_ref, xbc_ref, dt_ref, cw_ref, cb_ref, dtb_ref, a_ref, dsk_ref, nw_ref, o_ref,
                ext_ref, state_ref):
    j = pl.program_id(1)
    tm = xbc_ref.shape[0]
    halo = 8

    @pl.when(j == 0)
    def _():
        ext_ref[0:halo, :] = jnp.zeros((halo, CONV_CH), F32)
        state_ref[...] = jnp.zeros_like(state_ref)

    ext_ref[halo:halo + tm, :] = xbc_ref[...]

    c = SSD_CHUNK
    row_i = lax.broadcasted_iota(jnp.int32, (c, c), 0)
    col_i = lax.broadcasted_iota(jnp.int32, (c, c), 1)
    tril = row_i >= col_i
    tril_f = tril.astype(F32)
    lane256 = lax.broadcasted_iota(jnp.int32, (c, SSD_WIDTH), 1)
    head_even = lax.broadcasted_iota(jnp.int32, (c, LANES), 1) < SSD_HEAD_DIM
    a_row = -jnp.exp(a_ref[...])

    for ci in range(tm // c):
        base = halo + ci * c
        acc = cb_ref[...]
        for kk in range(CONV_WIDTH):
            shift = CONV_WIDTH - 1 - kk
            acc = acc + cw_ref[kk:kk + 1, :] * ext_ref[base - shift:base - shift + c, :]
        xc = acc * jax.nn.sigmoid(acc)
        xs = xc[:, :SSD_WIDTH]
        bm = xc[:, SSD_WIDTH:SSD_WIDTH + SSD_GROUPS * SSD_STATE]
        cm = xc[:, SSD_WIDTH + SSD_GROUPS * SSD_STATE:]

        dt = jax.nn.softplus(dt_ref[ci * c:(ci + 1) * c, :] + dtb_ref[...])
        adt = dt * a_row
        acs = jnp.dot(tril_f, adt, precision=lax.Precision.HIGHEST, preferred_element_type=F32)
        acs_t = acs.T
        a_last = acs[c - 1:c, :]

        dt_b = _per_head_lanes(dt, lane256)
        acs_b = _per_head_lanes(acs, lane256)
        alast_b = _per_head_lanes(a_last, lane256[:1])
        xdt = xs * dt_b
        xdt_bf = xdt.astype(BF16)
        xdec_bf = (xdt * jnp.exp(alast_b - acs_b)).astype(BF16)

        y_pairs, new_states = [], []
        for g in range(SSD_GROUPS):
            sl = slice(g * LANES, (g + 1) * LANES)
            bg = bm[:, sl]
            cg_bf = cm[:, sl].astype(BF16)
            gmat = _dot_nt(cg_bf, bg.astype(BF16))
            ys = []
            for hh in range(2):
                h = 2 * g + hh
                diff = acs[:, h:h + 1] - acs_t[h:h + 1, :]
                lmat = jnp.exp(jnp.where(tril, diff, NEG))
                ys.append(_dot((gmat * lmat).astype(BF16), xdt_bf[:, sl]))
            y_diag = jnp.where(head_even, ys[0], ys[1])
            y_off = _dot(cg_bf, state_ref[:, sl].astype(BF16))
            y_pairs.append((y_diag, y_off))
            new_states.append(_dot(bg.T.astype(BF16), xdec_bf[:, sl]))
        y_diag = jnp.concatenate([y_pairs[0][0], y_pairs[1][0]], axis=1)
        y_off = jnp.concatenate([y_pairs[0][1], y_pairs[1][1]], axis=1)
        state_ref[...] = state_ref[...] * jnp.exp(alast_b) + jnp.concatenate(new_states, axis=1)

        y = y_diag + y_off * jnp.exp(acs_b) + dsk_ref[...] * xs
        zc = z_ref[ci * c:(ci + 1) * c, :]
        y = y * (zc * jax.nn.sigmoid(zc))
        y = y * lax.rsqrt(jnp.mean(jnp.square(y), axis=-1, keepdims=True) + RMS_EPS) * nw_ref[...]
        o_ref[ci * c:(ci + 1) * c, :] = y.astype(o_ref.dtype)

    ext_ref[0:halo, :] = ext_ref[tm:tm + halo, :]


def _ssd(z, xbc, dt, conv_w, conv_b, dt_bias, a_log, d_skip, norm_w, batch, seq):
    t = z.shape[0]
    steps = seq // SSD_TM
    pad = lambda a: jnp.pad(a.astype(F32), (0, LANES - a.shape[0]))[None, :]
    row = lambda n: pl.BlockSpec((SSD_TM, n), lambda b, j: (b * steps + j, 0))
    const = lambda shape: pl.BlockSpec(shape, lambda b, j: (0, 0))
    return pl.pallas_call(
        _ssd_kernel,
        grid=(batch, steps),
        in_specs=[row(SSD_WIDTH), row(CONV_CH), row(LANES),
                  const((CONV_WIDTH, CONV_CH)), const((1, CONV_CH)), const((1, LANES)), const((1, LANES)),
                  const((1, SSD_WIDTH)), const((1, SSD_WIDTH))],
        out_specs=row(SSD_WIDTH),
        out_shape=jax.ShapeDtypeStruct((t, SSD_WIDTH), BF16),
        scratch_shapes=[pltpu.VMEM((SSD_TM + 8, CONV_CH), F32), pltpu.VMEM((SSD_STATE, SSD_WIDTH), F32)],
        compiler_params=_cparams(("parallel", "arbitrary")),
        name="ssd",
    )(z, xbc, dt, conv_w.astype(F32), conv_b.astype(F32)[None, :], pad(dt_bias), pad(a_log),
      jnp.repeat(d_skip.astype(F32), SSD_HEAD_DIM)[None, :], norm_w.astype(F32)[None, :])


def _mem_kernel(qm_ref, mem_ref, wkv_ref, mb_ref, o_ref, k_sc, v_sc):
    @pl.when(pl.program_id(1) == 0)
    def _():
        kv = _dot(mem_ref[0].astype(BF16), wkv_ref[...])
        k_sc[...] = kv[:, :MEM_WIDTH].astype(BF16)
        v_sc[...] = kv[:, MEM_WIDTH:].astype(BF16)

    tq = qm_ref.shape[0]
    head0 = lax.broadcasted_iota(jnp.int32, (tq, LANES), 1) < HEAD_DIM
    outs = []
    for hp in range(MEM_WIDTH // LANES):
        sl = slice(hp * LANES, (hp + 1) * LANES)
        qp = qm_ref[:, sl]
        kp = k_sc[:, sl]
        vp = v_sc[:, sl]
        res = []
        for hh in range(2):
            h = 2 * hp + hh
            keep = head0 if hh == 0 else jnp.logical_not(head0)
            qh = jnp.where(keep, qp, 0.0).astype(BF16)
            s = _dot_nt(qh, kp) + mb_ref[h:h + 1, :]
            m = jnp.max(s, axis=-1, keepdims=True)
            p = jnp.exp(s - m)
            den = jnp.sum(p, axis=-1, keepdims=True)
            res.append(_dot(p.astype(BF16), vp) / den)
        outs.append(jnp.where(head0, res[0], res[1]))
    o_ref[...] = jnp.concatenate(outs, axis=1).astype(o_ref.dtype)


def _memory_attention(qm, mem, wkv, mem_bias, batch, seq):
    t = qm.shape[0]
    steps = seq // MEM_TQ
    mb = jnp.pad(mem_bias.astype(F32), ((0, 8 - N_MEM_HEADS), (0, 0)))
    return pl.pallas_call(
        _mem_kernel,
        grid=(batch, steps),
        in_specs=[pl.BlockSpec((MEM_TQ, MEM_WIDTH), lambda b, j: (b * steps + j, 0)),
                  pl.BlockSpec((1, N_MEM, D_MODEL), lambda b, j: (b, 0, 0)),
                  pl.BlockSpec((D_MODEL, 2 * MEM_WIDTH), lambda b, j: (0, 0)),
                  pl.BlockSpec((8, N_MEM), lambda b, j: (0, 0))],
        out_specs=pl.BlockSpec((MEM_TQ, MEM_WIDTH), lambda b, j: (b * steps + j, 0)),
        out_shape=jax.ShapeDtypeStruct((t, MEM_WIDTH), BF16),
        scratch_shapes=[pltpu.VMEM((N_MEM, MEM_WIDTH), BF16), pltpu.VMEM((N_MEM, MEM_WIDTH), BF16)],
        compiler_params=_cparams(("parallel", "arbitrary")),
        name="mem_attn",
    )(qm, mem, wkv, mb)


def _layer_norm(h, g, b):
    mu = jnp.mean(h, axis=-1, keepdims=True)
    var = jnp.mean(jnp.square(h - mu), axis=-1, keepdims=True)
    return (h - mu) * lax.rsqrt(var + LN_EPS) * g + b


def _outproj_kernel(attn_ref, ssd_ref, memo_ref, x_ref, wo_ref, g_ref, b_ref, wr_ref, br_ref, tri_ref,
                    cast_in0, cast_in1, x1_ref, route_ref, cnt_ref, cast_out0, cast_out1, carry_ref):
    i = pl.program_id(0)
    _cast_slabs((cast_in0, cast_in1, cast_out0, cast_out1))

    @pl.when(i == 0)
    def _():
        carry_ref[...] = jnp.zeros_like(carry_ref)

    y = (_dot(attn_ref[...], wo_ref[0:ATTN_WIDTH, :])
         + _dot(ssd_ref[...], wo_ref[ATTN_WIDTH:ATTN_WIDTH + SSD_WIDTH, :])
         + _dot(memo_ref[...], wo_ref[ATTN_WIDTH + SSD_WIDTH:, :]))
    x1 = _layer_norm(ALPHA * _load_rows(x_ref) + y, g_ref[...], b_ref[...])
    _store_rows(x1_ref, x1)

    logits = lax.dot_general(wr_ref[...], x1, (((1,), (1,)), ((), ())), precision=lax.Precision.HIGHEST,
                             preferred_element_type=F32) + br_ref[...]
    mx = jnp.max(logits, axis=0, keepdims=True)
    ex = jnp.exp(logits - mx)
    sc = ex / jnp.sum(ex, axis=0, keepdims=True)
    s = [sc[e:e + 1, :] for e in range(N_EXPERTS)]

    def top2_sum(vals):
        best = None
        for a, b in PAIRS:
            pair = vals[a] + vals[b]
            best = pair if best is None else jnp.maximum(best, pair)
        return best

    grp = [top2_sum(s[4 * g:4 * g + 4]) for g in range(N_EXPERT_GROUPS)]
    best_g = jnp.zeros_like(grp[0], dtype=jnp.int32)
    best_v = grp[0]
    for g in range(1, N_EXPERT_GROUPS):
        upd = grp[g] > best_v
        best_g = jnp.where(upd, g, best_g)
        best_v = jnp.where(upd, grp[g], best_v)
    sel = []
    for jj in range(EXPERTS_PER_GROUP):
        v = s[3 * EXPERTS_PER_GROUP + jj]
        for g in range(N_EXPERT_GROUPS - 2, -1, -1):
            v = jnp.where(best_g == g, s[g * EXPERTS_PER_GROUP + jj], v)
        sel.append(v)

    def argmax_first(vals):
        idx = jnp.zeros_like(best_g)
        val = vals[0]
        for jj in range(1, len(vals)):
            upd = vals[jj] > val
            idx = jnp.where(upd, jj, idx)
            val = jnp.where(upd, vals[jj], val)
        return idx, val

    i1, _ = argmax_first(sel)
    i2, _ = argmax_first([jnp.where(i1 == jj, -1.0, sel[jj]) for jj in range(EXPERTS_PER_GROUP)])
    lo = jnp.minimum(i1, i2)
    hi = jnp.maximum(i1, i2)
    pair_idx = jnp.where(lo == 0, hi - 1, jnp.where(lo == 1, hi + 1, 5))
    cls = best_g * len(PAIRS) + pair_idx

    tm = cls.shape[1]
    onehot = (lax.broadcasted_iota(jnp.int32, (CLS_PAD, tm), 0) == cls)
    prefix = _dot(onehot.astype(BF16), tri_ref[...])
    oh = onehot.astype(F32)
    carry = carry_ref[:, 0:1]
    rank = jnp.sum(oh * (prefix + carry), axis=0, keepdims=True)
    new_carry = carry + jnp.sum(oh, axis=1, keepdims=True)
    carry_ref[...] = jnp.broadcast_to(new_carry, carry_ref.shape)
    cnt_ref[...] = jnp.broadcast_to(new_carry, cnt_ref.shape)

    route_ref[...] = jnp.concatenate([cls.astype(F32), rank, jnp.zeros((6, tm), F32)], axis=0)


def _outproj_route(attn, ssd, memo, x, wo, g, b, wr_t, br_col, tri, t, expert_ws, layer):
    row = lambda n: pl.BlockSpec((OUT_TM, n), lambda i: (i, 0))
    const = lambda shape: pl.BlockSpec(shape, lambda i: (0, 0))
    steps = t // OUT_TM
    casts = [_cast_slab_specs(w, layer, steps) for w in expert_ws]
    return pl.pallas_call(
        _outproj_kernel,
        grid=(steps,),
        in_specs=[row(ATTN_WIDTH), row(SSD_WIDTH), row(MEM_WIDTH), _row_spec(OUT_TM, x),
                  const((D_MODEL, D_MODEL)), const((1, D_MODEL)), const((1, D_MODEL)),
                  const((N_EXPERTS, D_MODEL)), const((N_EXPERTS, 1)), const((OUT_TM, OUT_TM))]
        + [c[0] for c in casts],
        out_specs=[pl.BlockSpec((OUT_TM,) + TOKEN_ROW, lambda i: (i,) + (0,) * len(TOKEN_ROW)),
                   pl.BlockSpec((8, OUT_TM), lambda i: (0, i)), const((CLS_PAD, LANES))]
        + [c[1] for c in casts],
        out_shape=[jax.ShapeDtypeStruct((t,) + TOKEN_ROW, F32), jax.ShapeDtypeStruct((8, t), F32),
                   jax.ShapeDtypeStruct((CLS_PAD, LANES), F32)] + [c[2] for c in casts],
        scratch_shapes=[pltpu.VMEM((CLS_PAD, LANES), F32)],
        compiler_params=_cparams(("arbitrary",)),
        name="outproj_route",
    )(attn, ssd, memo, x, wo, g, b, wr_t, br_col, tri, *expert_ws)


def _moe_kernel(ea_ref, eb_ref, nv_ref, tok_ref,
                x1_hbm, wga, wua, wda, wgb, wub, wdb, g_ref, b_ref, wr_ref, br_ref,
                out_hbm, xbuf0, xbuf1, obuf0, obuf1, gsem, ssem, *, n_tok):
    i = pl.program_id(0)
    n_used = nv_ref[pl.num_programs(0)]
    xbufs = (xbuf0, xbuf1)
    obufs = (obuf0, obuf1)

    def gather_copy(tok, r, p):
        src = x1_hbm.at[pl.ds(jnp.minimum(tok, n_tok - 1), 1)]
        return pltpu.make_async_copy(src, xbufs[p].at[pl.ds(r, 1)], gsem.at[p])

    def scatter_copy(dst, r, p):
        return pltpu.make_async_copy(obufs[p].at[pl.ds(r, 1)], out_hbm.at[pl.ds(dst, 1)], ssem.at[p])

    def tile_gather_done(p):
        return pltpu.make_async_copy(x1_hbm.at[pl.ds(0, MOE_TM)], xbufs[p], gsem.at[p])

    def tile_scatter_done(p):
        return pltpu.make_async_copy(obufs[p], out_hbm.at[pl.ds(0, MOE_TM)], ssem.at[p])

    @pl.when(i == 0)
    def _():
        for p in range(2):
            obufs[p][...] = jnp.zeros_like(obufs[p])
            spare = pltpu.make_async_copy(obufs[p], out_hbm.at[pl.ds(n_tok + p * MOE_TM, MOE_TM)], ssem.at[p])
            spare.start()
            spare.wait()

        def first(r, carry):
            gather_copy(tok_ref[r], r, 0).start()
            return carry

        lax.fori_loop(0, MOE_TM, first, 0)

    def step(p):
        q = 1 - p
        tile_gather_done(p).wait()
        next_base = jnp.minimum(i + 1, n_used - 1) * MOE_TM
        prev_base = jnp.maximum(i - 1, 0) * MOE_TM
        is_first = i == 0

        def issue_rows(lo, hi):
            for r in range(lo, hi):
                gather_copy(tok_ref[next_base + r], r, q).start()
                dst = jnp.where(is_first, n_tok + q * MOE_TM + r, tok_ref[prev_base + r])
                scatter_copy(dst, r, q).start()

        @pl.when(n_used > 0)
        def _():
            issue_rows(0, MOE_TM)

        x = _load_rows(xbufs[p])
        xb = x.astype(BF16)
        ea = ea_ref[i]
        eb = eb_ref[i]
        w_diff = wr_ref[pl.ds(eb, 1), :] - wr_ref[pl.ds(ea, 1), :]
        logit_diff = (jnp.sum(x * w_diff, axis=-1, keepdims=True)
                      + (br_ref[pl.ds(eb, 1), :] - br_ref[pl.ds(ea, 1), :]))
        gates = (jax.nn.sigmoid(-logit_diff), jax.nn.sigmoid(logit_diff))
        half = D_EXPERT // 2
        y = None
        for e, (wg, wu, wd) in enumerate(((wga, wua, wda), (wgb, wub, wdb))):
            acc = None
            for c in range(2):
                cs = slice(c * half, (c + 1) * half)
                gate = _dot(xb, wg[0, :, cs])
                up = _dot(xb, wu[0, :, cs])
                hid = (gate * jax.nn.sigmoid(gate) * up).astype(BF16)
                part = _dot(hid, wd[0, cs, :])
                acc = part if acc is None else acc + part
            term = gates[e] * acc
            y = term if y is None else y + term
        out = _layer_norm(ALPHA * x + y, g_ref[...], b_ref[...])
        tile_scatter_done(q).wait()
        _store_rows(obufs[p], out)

        @pl.when(i == n_used - 1)
        def _():
            base = i * MOE_TM

            def last(r, carry):
                scatter_copy(tok_ref[base + r], r, p).start()
                return carry

            lax.fori_loop(0, MOE_TM, last, 0)
            tile_scatter_done(p).wait()
            tile_gather_done(q).wait()

    for p in range(2):
        @pl.when(jnp.logical_and(i < n_used, i % 2 == p))
        def _(p=p):
            step(p)


def _moe(x1, tok_of_slot, tile_ea, tile_eb, tile_nv, wg, wu, wd, g, b, wr_t, br_col, n_tok):
    nt = tile_ea.shape[0]
    wspec = lambda which: pl.BlockSpec(
        (1, D_MODEL, D_EXPERT), lambda i, ea, eb, nv, tok: ((ea if which == 0 else eb)[i], 0, 0))
    const = lambda shape: pl.BlockSpec(shape, lambda i, ea, eb, nv, tok: (0, 0))
    tile_buf = pltpu.VMEM((MOE_TM,) + TOKEN_ROW, F32)
    grid_spec = pltpu.PrefetchScalarGridSpec(
        num_scalar_prefetch=4,
        grid=(nt,),
        in_specs=[pl.BlockSpec(memory_space=pl.ANY),
                  wspec(0), wspec(0), wspec(0), wspec(1), wspec(1), wspec(1),
                  const((1, D_MODEL)), const((1, D_MODEL)), const((N_EXPERTS, D_MODEL)), const((N_EXPERTS, 1))],
        out_specs=pl.BlockSpec(memory_space=pl.ANY),
        scratch_shapes=[tile_buf, tile_buf, tile_buf, tile_buf,
                        pltpu.SemaphoreType.DMA((2,)), pltpu.SemaphoreType.DMA((2,))],
    )
    return pl.pallas_call(
        functools.partial(_moe_kernel, n_tok=n_tok),
        grid_spec=grid_spec,
        out_shape=jax.ShapeDtypeStruct((n_tok + 2 * MOE_TM,) + TOKEN_ROW, F32),
        compiler_params=_cparams(("arbitrary",)),
        name="moe_experts",
    )(tile_ea, tile_eb, tile_nv, tok_of_slot, x1, wg, wu, wd, wg, wu, wd, g, b, wr_t, br_col)


def _routing_tables(route, counts, t):
    n_tiles = t // MOE_TM + N_CLASSES
    cls = route[0].astype(jnp.int32)
    rank = route[1].astype(jnp.int32)
    cnt = counts[:N_CLASSES, 0].astype(jnp.int32)
    tiles_per_cls = (cnt + MOE_TM - 1) // MOE_TM
    tile_end = jnp.cumsum(tiles_per_cls)
    tile_start = tile_end - tiles_per_cls
    dest = tile_start[cls] * MOE_TM + rank
    slot = jnp.arange(n_tiles * MOE_TM, dtype=jnp.int32)
    spare_row = t + ((slot // MOE_TM) % 2) * MOE_TM + slot % MOE_TM
    tok_of_slot = spare_row.at[dest].set(jnp.arange(t, dtype=jnp.int32))
    tile_id = jnp.arange(n_tiles, dtype=jnp.int32)
    n_used = tile_end[-1]
    tile_cls = jnp.minimum(jnp.searchsorted(tile_end, jnp.minimum(tile_id, n_used - 1), side="right"),
                           N_CLASSES - 1).astype(jnp.int32)
    tile_nv = jnp.where(tile_id < n_used,
                        jnp.clip(cnt[tile_cls] - (tile_id - tile_start[tile_cls]) * MOE_TM, 0, MOE_TM), 0)
    pair_lo = jnp.asarray([p[0] for p in PAIRS], jnp.int32)
    pair_hi = jnp.asarray([p[1] for p in PAIRS], jnp.int32)
    grp_base = (tile_cls // len(PAIRS)) * EXPERTS_PER_GROUP
    tile_ea = grp_base + pair_lo[tile_cls % len(PAIRS)]
    tile_eb = grp_base + pair_hi[tile_cls % len(PAIRS)]
    return tok_of_slot, tile_ea, tile_eb, jnp.concatenate([tile_nv, n_used[None]]).astype(jnp.int32)


def _rearranged_w_in(w):
    scale = HEAD_DIM ** -0.5
    s = np.cumsum([ATTN_WIDTH, ATTN_WIDTH, ATTN_WIDTH, SSD_WIDTH, CONV_CH, N_SSD_HEADS]).tolist()
    q, k, v, z, xbc, dt, qm = (w[:, a:b] for a, b in zip([0] + s, s + [w.shape[1]]))
    dt = jnp.pad(dt, ((0, 0), (0, LANES - N_SSD_HEADS)))
    return jnp.concatenate([q * scale, k, v, z, xbc, qm * scale, dt], axis=1).astype(BF16)


def kernel(x, mem, w_in, w_out, rel_bias, conv_w, conv_b, dt_bias, a_log, d_skip, ssd_norm_w, w_mem_kv, mem_bias,
           ln1_g, ln1_b, ln2_g, ln2_b, w_router, b_router, w_gate, w_up, w_down):
    batch, seq, _ = x.shape
    t = batch * seq
    assert seq % (16 * ATTN_BLK) == 0 and seq % SSD_TM == 0 and t % MOE_TM == 0
    bias = _attn_bias_tables(rel_bias)
    tri = jnp.asarray(np.triu(np.ones((OUT_TM, OUT_TM), np.float32), k=1), BF16)
    wr_t = w_router.astype(F32).T
    br_col = b_router.astype(F32)[:, None]
    h = x.reshape(t, D_MODEL)
    for l in range(DEPTH):
        q, k, v, z, xbc, qm, dt, wg = _inproj(h, _rearranged_w_in(w_in[l]), t, w_gate, l)
        attn = _attention(q, k, v, bias, batch, seq)
        ssd = _ssd(z, xbc, dt, conv_w[l], conv_b[l], dt_bias[l], a_log[l], d_skip[l], ssd_norm_w[l], batch, seq)
        memo = _memory_attention(qm, mem, w_mem_kv[l].astype(BF16), mem_bias[l], batch, seq)
        x1, route, counts, wu, wd = _outproj_route(attn, ssd, memo, h, w_out[l].astype(BF16),
                                                   ln1_g[l][None, :], ln1_b[l][None, :], wr_t, br_col, tri, t,
                                                   (w_up, w_down), l)
        tok_of_slot, tile_ea, tile_eb, tile_nv = _routing_tables(route, counts, t)
        h = _moe(x1, tok_of_slot, tile_ea, tile_eb, tile_nv,
                 wg, wu, wd, ln2_g[l][None, :], ln2_b[l][None, :], wr_t, br_col, t)
    return h[:t].reshape(batch, seq, D_MODEL)
```

```python
import functools
import math

import numpy as np
import jax
import jax.numpy as jnp
from jax import lax
from jax.experimental import pallas as pl
from jax.experimental.pallas import tpu as pltpu

F32 = jnp.float32
BF16 = jnp.bfloat16

D_MODEL = 1024
HEAD_DIM = 64
N_ATTN_HEADS = 8
ATTN_WIDTH = N_ATTN_HEADS * HEAD_DIM
DILATED_BRANCHES = ((128, 1), (512, 4), (2048, 16))
ATTN_BLK = 128
ATTN_UNROLL = 8
N_SSD_HEADS = 4
SSD_HEAD_DIM = 64
SSD_WIDTH = N_SSD_HEADS * SSD_HEAD_DIM
SSD_GROUPS = 2
SSD_STATE = 128
CONV_WIDTH = 4
SSD_CHUNK = 128
CONV_CH = SSD_WIDTH + 2 * SSD_GROUPS * SSD_STATE
N_MEM_HEADS = 4
MEM_WIDTH = N_MEM_HEADS * HEAD_DIM
N_MEM = 256
N_BUCKETS = 32
MAX_DISTANCE = 2048
N_EXPERTS = 16
N_EXPERT_GROUPS = 4
EXPERTS_PER_GROUP = 4
D_EXPERT = 1024
DEPTH = 2
ALPHA = (2 * DEPTH) ** 0.25
LN_EPS = 1e-5
RMS_EPS = 1e-5

LANES = 128
NEG = -1e30
PAIRS = ((0, 1), (0, 2), (0, 3), (1, 2), (1, 3), (2, 3))
N_CLASSES = N_EXPERT_GROUPS * len(PAIRS)
CLS_PAD = 32

_C_Q, _C_K, _C_V = 0, ATTN_WIDTH, 2 * ATTN_WIDTH
_C_Z = 3 * ATTN_WIDTH
_C_XBC = _C_Z + SSD_WIDTH
_C_QM = _C_XBC + CONV_CH
_C_DT = _C_QM + MEM_WIDTH
IN_COLS = _C_DT + LANES
TOKEN_ROW = (D_MODEL,)

IN_TM = 512
MEM_TQ = 512
SSD_TM = 512
OUT_TM = 512
MOE_TM = 256
MOE_ROW_GROUP = 32
VMEM_LIMIT = 56 * 1024 * 1024


def _cparams(sem):
    return pltpu.CompilerParams(dimension_semantics=sem, vmem_limit_bytes=VMEM_LIMIT)


def _dot(a, b):
    return jnp.dot(a, b, preferred_element_type=F32)


def _dot_nt(a, b):
    return lax.dot_general(a, b, (((1,), (1,)), ((), ())), preferred_element_type=F32)


def _load_rows(ref):
    if len(ref.shape) == 2:
        return ref[...]
    return jnp.concatenate([ref[:, k, :] for k in range(ref.shape[1])], axis=1)


def _store_rows(ref, val):
    if len(ref.shape) == 2:
        ref[...] = val
        return
    for k in range(ref.shape[1]):
        ref[:, k, :] = val[:, k * LANES:(k + 1) * LANES]


def _row_spec(tm, x):
    if x.ndim == 2:
        return pl.BlockSpec((tm, x.shape[1]), lambda i: (i, 0))
    return pl.BlockSpec((tm,) + x.shape[1:], lambda i: (i, 0, 0))


def _cast_slab_specs(w, layer, steps):
    _, n_exp, kk, nn = w.shape
    if steps >= n_exp:
        per = steps // n_exp
        rows = kk // per
        in_spec = pl.BlockSpec((1, 1, rows, nn), lambda i: (layer, i // per, i % per, 0))
        out_spec = pl.BlockSpec((1, rows, nn), lambda i: (i // per, i % per, 0))
    else:
        per = n_exp // steps
        in_spec = pl.BlockSpec((1, per, kk, nn), lambda i: (layer, i, 0, 0))
        out_spec = pl.BlockSpec((per, kk, nn), lambda i: (i, 0, 0))
    return in_spec, out_spec, jax.ShapeDtypeStruct((n_exp, kk, nn), BF16)


def _cast_slabs(refs):
    n = len(refs) // 2
    for w_ref, o_ref in zip(refs[:n], refs[n:]):
        o_ref[...] = w_ref[0].astype(o_ref.dtype)


def _inproj_kernel(x_ref, w_ref, cast_in, q_ref, k_ref, v_ref, z_ref, xbc_ref, qm_ref, dt_ref, cast_out):
    xb = _load_rows(x_ref).astype(BF16)
    for ref, lo, hi in ((q_ref, _C_Q, _C_K), (k_ref, _C_K, _C_V), (v_ref, _C_V, _C_Z), (z_ref, _C_Z, _C_XBC),
                        (xbc_ref, _C_XBC, _C_QM), (qm_ref, _C_QM, _C_DT), (dt_ref, _C_DT, IN_COLS)):
        ref[...] = _dot(xb, w_ref[:, lo:hi]).astype(ref.dtype)
    _cast_slabs((cast_in, cast_out))


def _inproj(x, w, t, expert_w, layer):
    widths = (ATTN_WIDTH, ATTN_WIDTH, ATTN_WIDTH, SSD_WIDTH, CONV_CH, MEM_WIDTH, LANES)
    steps = t // IN_TM
    cast_in, cast_out, cast_shape = _cast_slab_specs(expert_w, layer, steps)
    return pl.pallas_call(
        _inproj_kernel,
        grid=(steps,),
        in_specs=[_row_spec(IN_TM, x), pl.BlockSpec((D_MODEL, IN_COLS), lambda i: (0, 0)), cast_in],
        out_specs=[pl.BlockSpec((IN_TM, wd), lambda i: (i, 0)) for wd in widths] + [cast_out],
        out_shape=[jax.ShapeDtypeStruct((t, wd), F32) for wd in widths] + [cast_shape],
        compiler_params=_cparams(("parallel",)),
        name="inproj",
    )(x, w, expert_w)


def _attn_kernel(q_ref, k_ref, v_ref, bias_ref, o_ref, obuf, mbuf, dbuf, *, seq):
    def attend(bi, d, qs, ks, qrows, bias_of_head):
        head0 = lax.broadcasted_iota(jnp.int32, (qrows, LANES), 1) < HEAD_DIM
        qb = q_ref[0, pl.ds(qs, qrows, stride=d), :]
        kb = k_ref[0, pl.ds(ks, 2 * ATTN_BLK, stride=d), :].astype(BF16)
        vb = v_ref[0, pl.ds(ks, 2 * ATTN_BLK, stride=d), :].astype(BF16)
        outs, maxes, dens = [], [], []
        for h in range(2):
            keep = head0 if h == 0 else jnp.logical_not(head0)
            qh = jnp.where(keep, qb, 0.0).astype(BF16)
            s = _dot_nt(qh, kb) + bias_of_head(h)
            m = jnp.max(s, axis=-1, keepdims=True)
            p = jnp.exp(s - m)
            outs.append(_dot(p.astype(BF16), vb))
            maxes.append(m)
            dens.append(jnp.sum(p, axis=-1, keepdims=True))
        rows = pl.ds(qs, qrows, stride=d)
        obuf[bi, rows, :] = jnp.where(head0, outs[0], outs[1])
        mbuf[bi, rows, :] = jnp.where(head0, maxes[0], maxes[1])
        dbuf[bi, rows, :] = jnp.where(head0, dens[0], dens[1])

    for bi, (_, d) in enumerate(DILATED_BRANCHES):
        nb = seq // d // ATTN_BLK
        if nb == 2:
            def pair(r, carry, bi=bi, d=d):
                attend(bi, d, r, r, 2 * ATTN_BLK,
                       lambda h: jnp.concatenate([bias_ref[bi, 1, h], bias_ref[bi, 0, h]], axis=0))
                return carry

            lax.fori_loop(0, d, pair, 0, unroll=ATTN_UNROLL // 2)
            continue

        def block(idx, carry, bi=bi, d=d, nb=nb):
            r = idx // nb
            n = idx % nb
            first = 1 - jnp.minimum(n, 1)
            attend(bi, d, r + d * ATTN_BLK * n, r + d * ATTN_BLK * jnp.maximum(n - 1, 0), ATTN_BLK,
                   lambda h: bias_ref[bi, first, h])
            return carry

        lax.fori_loop(0, d * nb, block, 0, unroll=ATTN_UNROLL)

    rows = 256

    def merge(i, carry):
        sl = pl.ds(pl.multiple_of(i * rows, rows), rows)
        m0, m1, m2 = mbuf[0, sl, :], mbuf[1, sl, :], mbuf[2, sl, :]
        m = jnp.maximum(jnp.maximum(m0, m1), m2)
        w0, w1, w2 = jnp.exp(m0 - m), jnp.exp(m1 - m), jnp.exp(m2 - m)
        num = w0 * obuf[0, sl, :] + w1 * obuf[1, sl, :] + w2 * obuf[2, sl, :]
        den = w0 * dbuf[0, sl, :] + w1 * dbuf[1, sl, :] + w2 * dbuf[2, sl, :]
        o_ref[0, sl, :] = (num / den).astype(o_ref.dtype)
        return carry

    lax.fori_loop(0, seq // rows, merge, 0)


def _attention(q, k, v, bias, batch, seq):
    q3, k3, v3 = (a.reshape(batch, seq, ATTN_WIDTH) for a in (q, k, v))
    n_pairs = ATTN_WIDTH // LANES
    blk = pl.BlockSpec((1, seq, LANES), lambda b, hp: (b, 0, hp))
    out = pl.pallas_call(
        functools.partial(_attn_kernel, seq=seq),
        grid=(batch, n_pairs),
        in_specs=[blk, blk, blk,
                  pl.BlockSpec((3, 2, 2, ATTN_BLK, 2 * ATTN_BLK), lambda b, hp: (0, 0, hp, 0, 0))],
        out_specs=blk,
        out_shape=jax.ShapeDtypeStruct((batch, seq, ATTN_WIDTH), BF16),
        scratch_shapes=[pltpu.VMEM((3, seq, LANES), F32)] * 3,
        compiler_params=_cparams(("parallel", "parallel")),
        name="dilated_attn",
    )(q3, k3, v3, bias)
    return out.reshape(batch * seq, ATTN_WIDTH)


def _t5_bucket_np(dist):
    n = np.maximum(dist, 0)
    max_exact = N_BUCKETS // 2
    nf = np.maximum(n, max_exact).astype(np.float32)
    large = max_exact + (np.log(nf / np.float32(max_exact)) / np.float32(math.log(MAX_DISTANCE / max_exact))
                         * np.float32(N_BUCKETS - max_exact)).astype(np.int32)
    large = np.minimum(large, N_BUCKETS - 1)
    return np.where(n < max_exact, n, large)


def _attn_bias_tables(rel_bias):
    period = 3 * ATTN_BLK
    steps = np.arange(ATTN_BLK + 1)
    tables = []
    for _, d in DILATED_BRANCHES:
        onehot = np.eye(N_BUCKETS, dtype=np.float32)[_t5_bucket_np(steps * d)]
        tbl = jnp.dot(jnp.asarray(onehot), rel_bias.astype(F32), precision=lax.Precision.HIGHEST)
        masked = lambda n: jnp.full((n, N_ATTN_HEADS), NEG, F32)
        v_full = jnp.concatenate([tbl[::-1], masked(period - ATTN_BLK - 1)], axis=0)
        v_first = jnp.concatenate([tbl[:1], masked(period - ATTN_BLK), tbl[1:ATTN_BLK][::-1]], axis=0)
        variants = []
        for vec in (v_full, v_first):
            flat = jnp.tile(vec, (ATTN_BLK, 1))[:ATTN_BLK * (period - 1)]
            mat = flat.reshape(ATTN_BLK, period - 1, N_ATTN_HEADS)[:, :2 * ATTN_BLK]
            variants.append(mat.transpose(2, 0, 1))
        tables.append(jnp.stack(variants))
    return jnp.stack(tables)


def _per_head_lanes(cols, lane_idx):
    out = cols[:, N_SSD_HEADS - 1:N_SSD_HEADS]
    for h in range(N_SSD_HEADS - 2, -1, -1):
        out = jnp.where(lane_idx < (h + 1) * SSD_HEAD_DIM, cols[:, h:h + 1], out)
    return out


def _ssd_kernel(z_ref, xbc_ref, dt_ref, cw_ref, cb_ref, dtb_ref, a_ref, dsk_ref, nw_ref, o_ref,
                ext_ref, state_ref):
    j = pl.program_id(1)
    tm = xbc_ref.shape[0]
    halo = 8

    @pl.when(j == 0)
    def _():
        ext_ref[0:halo, :] = jnp.zeros((halo, CONV_CH), F32)
        state_ref[...] = jnp.zeros_like(state_ref)

    ext_ref[halo:halo + tm, :] = xbc_ref[...]

    c = SSD_CHUNK
    row_i = lax.broadcasted_iota(jnp.int32, (c, c), 0)
    col_i = lax.broadcasted_iota(jnp.int32, (c, c), 1)
    tril = row_i >= col_i
    tril_f = tril.astype(F32)
    lane256 = lax.broadcasted_iota(jnp.int32, (c, SSD_WIDTH), 1)
    head_even = lax.broadcasted_iota(jnp.int32, (c, LANES), 1) < SSD_HEAD_DIM
    a_row = -jnp.exp(a_ref[...])

    for ci in range(tm // c):
        base = halo + ci * c
        acc = cb_ref[...]
        for kk in range(CONV_WIDTH):
            shift = CONV_WIDTH - 1 - kk
            acc = acc + cw_ref[kk:kk + 1, :] * ext_ref[base - shift:base - shift + c, :]
        xc = acc * jax.nn.sigmoid(acc)
        xs = xc[:, :SSD_WIDTH]
        bm = xc[:, SSD_WIDTH:SSD_WIDTH + SSD_GROUPS * SSD_STATE]
        cm = xc[:, SSD_WIDTH + SSD_GROUPS * SSD_STATE:]

        dt = jax.nn.softplus(dt_ref[ci * c:(ci + 1) * c, :] + dtb_ref[...])
        adt = dt * a_row
        acs = jnp.dot(tril_f, adt, precision=lax.Precision.HIGHEST, preferred_element_type=F32)
        acs_t = acs.T
        a_last = acs[c - 1:c, :]

        dt_b = _per_head_lanes(dt, lane256)
        acs_b = _per_head_lanes(acs, lane256)
        alast_b = _per_head_lanes(a_last, lane256[:1])
        xdt = xs * dt_b
        xdt_bf = xdt.astype(BF16)
        xdec_bf = (xdt * jnp.exp(alast_b - acs_b)).astype(BF16)

        y_pairs, new_states = [], []
        for g in range(SSD_GROUPS):
            sl = slice(g * LANES, (g + 1) * LANES)
            bg = bm[:, sl]
            cg_bf = cm[:, sl].astype(BF16)
            gmat = _dot_nt(cg_bf, bg.astype(BF16))
            ys = []
            for hh in range(2):
                h = 2 * g + hh
                diff = acs[:, h:h + 1] - acs_t[h:h + 1, :]
                lmat = jnp.exp(jnp.where(tril, diff, NEG))
                ys.append(_dot((gmat * lmat).astype(BF16), xdt_bf[:, sl]))
            y_diag = jnp.where(head_even, ys[0], ys[1])
            y_off = _dot(cg_bf, state_ref[:, sl].astype(BF16))
            y_pairs.append((y_diag, y_off))
            new_states.append(_dot(bg.T.astype(BF16), xdec_bf[:, sl]))
        y_diag = jnp.concatenate([y_pairs[0][0], y_pairs[1][0]], axis=1)
        y_off = jnp.concatenate([y_pairs[0][1], y_pairs[1][1]], axis=1)
        state_ref[...] = state_ref[...] * jnp.exp(alast_b) + jnp.concatenate(new_states, axis=1)

        y = y_diag + y_off * jnp.exp(acs_b) + dsk_ref[...] * xs
        zc = z_ref[ci * c:(ci + 1) * c, :]
        y = y * (zc * jax.nn.sigmoid(zc))
        y = y * lax.rsqrt(jnp.mean(jnp.square(y), axis=-1, keepdims=True) + RMS_EPS) * nw_ref[...]
        o_ref[ci * c:(ci + 1) * c, :] = y.astype(o_ref.dtype)

    ext_ref[0:halo, :] = ext_ref[tm:tm + halo, :]


def _ssd(z, xbc, dt, conv_w, conv_b, dt_bias, a_log, d_skip, norm_w, batch, seq):
    t = z.shape[0]
    steps = seq // SSD_TM
    pad = lambda a: jnp.pad(a.astype(F32), (0, LANES - a.shape[0]))[None, :]
    row = lambda n: pl.BlockSpec((SSD_TM, n), lambda b, j: (b * steps + j, 0))
    const = lambda shape: pl.BlockSpec(shape, lambda b, j: (0, 0))
    return pl.pallas_call(
        _ssd_kernel,
        grid=(batch, steps),
        in_specs=[row(SSD_WIDTH), row(CONV_CH), row(LANES),
                  const((CONV_WIDTH, CONV_CH)), const((1, CONV_CH)), const((1, LANES)), const((1, LANES)),
                  const((1, SSD_WIDTH)), const((1, SSD_WIDTH))],
        out_specs=row(SSD_WIDTH),
        out_shape=jax.ShapeDtypeStruct((t, SSD_WIDTH), BF16),
        scratch_shapes=[pltpu.VMEM((SSD_TM + 8, CONV_CH), F32), pltpu.VMEM((SSD_STATE, SSD_WIDTH), F32)],
        compiler_params=_cparams(("parallel", "arbitrary")),
        name="ssd",
    )(z, xbc, dt, conv_w.astype(F32), conv_b.astype(F32)[None, :], pad(dt_bias), pad(a_log),
      jnp.repeat(d_skip.astype(F32), SSD_HEAD_DIM)[None, :], norm_w.astype(F32)[None, :])


def _mem_kernel(qm_ref, mem_ref, wkv_ref, mb_ref, o_ref, k_sc, v_sc):
    @pl.when(pl.program_id(1) == 0)
    def _():
        kv = _dot(mem_ref[0].astype(BF16), wkv_ref[...])
        k_sc[...] = kv[:, :MEM_WIDTH].astype(BF16)
        v_sc[...] = kv[:, MEM_WIDTH:].astype(BF16)

    tq = qm_ref.shape[0]
    head0 = lax.broadcasted_iota(jnp.int32, (tq, LANES), 1) < HEAD_DIM
    outs = []
    for hp in range(MEM_WIDTH // LANES):
        sl = slice(hp * LANES, (hp + 1) * LANES)
        qp = qm_ref[:, sl]
        kp = k_sc[:, sl]
        vp = v_sc[:, sl]
        res = []
        for hh in range(2):
            h = 2 * hp + hh
            keep = head0 if hh == 0 else jnp.logical_not(head0)
            qh = jnp.where(keep, qp, 0.0).astype(BF16)
            s = _dot_nt(qh, kp) + mb_ref[h:h + 1, :]
            m = jnp.max(s, axis=-1, keepdims=True)
            p = jnp.exp(s - m)
            den = jnp.sum(p, axis=-1, keepdims=True)
            res.append(_dot(p.astype(BF16), vp) / den)
        outs.append(jnp.where(head0, res[0], res[1]))
    o_ref[...] = jnp.concatenate(outs, axis=1).astype(o_ref.dtype)


def _memory_attention(qm, mem, wkv, mem_bias, batch, seq):
    t = qm.shape[0]
    steps = seq // MEM_TQ
    mb = jnp.pad(mem_bias.astype(F32), ((0, 8 - N_MEM_HEADS), (0, 0)))
    return pl.pallas_call(
        _mem_kernel,
        grid=(batch, steps),
        in_specs=[pl.BlockSpec((MEM_TQ, MEM_WIDTH), lambda b, j: (b * steps + j, 0)),
                  pl.BlockSpec((1, N_MEM, D_MODEL), lambda b, j: (b, 0, 0)),
                  pl.BlockSpec((D_MODEL, 2 * MEM_WIDTH), lambda b, j: (0, 0)),
                  pl.BlockSpec((8, N_MEM), lambda b, j: (0, 0))],
        out_specs=pl.BlockSpec((MEM_TQ, MEM_WIDTH), lambda b, j: (b * steps + j, 0)),
        out_shape=jax.ShapeDtypeStruct((t, MEM_WIDTH), BF16),
        scratch_shapes=[pltpu.VMEM((N_MEM, MEM_WIDTH), BF16), pltpu.VMEM((N_MEM, MEM_WIDTH), BF16)],
        compiler_params=_cparams(("parallel", "arbitrary")),
        name="mem_attn",
    )(qm, mem, wkv, mb)


def _layer_norm(h, g, b):
    mu = jnp.mean(h, axis=-1, keepdims=True)
    var = jnp.mean(jnp.square(h - mu), axis=-1, keepdims=True)
    return (h - mu) * lax.rsqrt(var + LN_EPS) * g + b


def _outproj_kernel(attn_ref, ssd_ref, memo_ref, x_ref, wo_ref, g_ref, b_ref, wr_ref, br_ref, tri_ref,
                    cast_in0, cast_in1, x1_ref, route_ref, cnt_ref, cast_out0, cast_out1, carry_ref):
    i = pl.program_id(0)
    _cast_slabs((cast_in0, cast_in1, cast_out0, cast_out1))

    @pl.when(i == 0)
    def _():
        carry_ref[...] = jnp.zeros_like(carry_ref)

    y = (_dot(attn_ref[...], wo_ref[0:ATTN_WIDTH, :])
         + _dot(ssd_ref[...], wo_ref[ATTN_WIDTH:ATTN_WIDTH + SSD_WIDTH, :])
         + _dot(memo_ref[...], wo_ref[ATTN_WIDTH + SSD_WIDTH:, :]))
    x1 = _layer_norm(ALPHA * _load_rows(x_ref) + y, g_ref[...], b_ref[...])
    _store_rows(x1_ref, x1)

    logits = lax.dot_general(wr_ref[...], x1, (((1,), (1,)), ((), ())), precision=lax.Precision.HIGHEST,
                             preferred_element_type=F32) + br_ref[...]
    mx = jnp.max(logits, axis=0, keepdims=True)
    ex = jnp.exp(logits - mx)
    sc = ex / jnp.sum(ex, axis=0, keepdims=True)
    s = [sc[e:e + 1, :] for e in range(N_EXPERTS)]

    def top2_sum(vals):
        best = None
        for a, b in PAIRS:
            pair = vals[a] + vals[b]
            best = pair if best is None else jnp.maximum(best, pair)
        return best

    grp = [top2_sum(s[4 * g:4 * g + 4]) for g in range(N_EXPERT_GROUPS)]
    best_g = jnp.zeros_like(grp[0], dtype=jnp.int32)
    best_v = grp[0]
    for g in range(1, N_EXPERT_GROUPS):
        upd = grp[g] > best_v
        best_g = jnp.where(upd, g, best_g)
        best_v = jnp.where(upd, grp[g], best_v)
    sel = []
    for jj in range(EXPERTS_PER_GROUP):
        v = s[3 * EXPERTS_PER_GROUP + jj]
        for g in range(N_EXPERT_GROUPS - 2, -1, -1):
            v = jnp.where(best_g == g, s[g * EXPERTS_PER_GROUP + jj], v)
        sel.append(v)

    def argmax_first(vals):
        idx = jnp.zeros_like(best_g)
        val = vals[0]
        for jj in range(1, len(vals)):
            upd = vals[jj] > val
            idx = jnp.where(upd, jj, idx)
            val = jnp.where(upd, vals[jj], val)
        return idx, val

    i1, _ = argmax_first(sel)
    i2, _ = argmax_first([jnp.where(i1 == jj, -1.0, sel[jj]) for jj in range(EXPERTS_PER_GROUP)])
    lo = jnp.minimum(i1, i2)
    hi = jnp.maximum(i1, i2)
    pair_idx = jnp.where(lo == 0, hi - 1, jnp.where(lo == 1, hi + 1, 5))
    cls = best_g * len(PAIRS) + pair_idx

    tm = cls.shape[1]
    onehot = (lax.broadcasted_iota(jnp.int32, (CLS_PAD, tm), 0) == cls)
    prefix = _dot(onehot.astype(BF16), tri_ref[...])
    oh = onehot.astype(F32)
    carry = carry_ref[:, 0:1]
    rank = jnp.sum(oh * (prefix + carry), axis=0, keepdims=True)
    new_carry = carry + jnp.sum(oh, axis=1, keepdims=True)
    carry_ref[...] = jnp.broadcast_to(new_carry, carry_ref.shape)
    cnt_ref[...] = jnp.broadcast_to(new_carry, cnt_ref.shape)

    route_ref[...] = jnp.concatenate([cls.astype(F32), rank, jnp.zeros((6, tm), F32)], axis=0)


def _outproj_route(attn, ssd, memo, x, wo, g, b, wr_t, br_col, tri, t, expert_ws, layer):
    row = lambda n: pl.BlockSpec((OUT_TM, n), lambda i: (i, 0))
    const = lambda shape: pl.BlockSpec(shape, lambda i: (0, 0))
    steps = t // OUT_TM
    casts = [_cast_slab_specs(w, layer, steps) for w in expert_ws]
    return pl.pallas_call(
        _outproj_kernel,
        grid=(steps,),
        in_specs=[row(ATTN_WIDTH), row(SSD_WIDTH), row(MEM_WIDTH), _row_spec(OUT_TM, x),
                  const((D_MODEL, D_MODEL)), const((1, D_MODEL)), const((1, D_MODEL)),
                  const((N_EXPERTS, D_MODEL)), const((N_EXPERTS, 1)), const((OUT_TM, OUT_TM))]
        + [c[0] for c in casts],
        out_specs=[pl.BlockSpec((OUT_TM,) + TOKEN_ROW, lambda i: (i,) + (0,) * len(TOKEN_ROW)),
                   pl.BlockSpec((8, OUT_TM), lambda i: (0, i)), const((CLS_PAD, LANES))]
        + [c[1] for c in casts],
        out_shape=[jax.ShapeDtypeStruct((t,) + TOKEN_ROW, F32), jax.ShapeDtypeStruct((8, t), F32),
                   jax.ShapeDtypeStruct((CLS_PAD, LANES), F32)] + [c[2] for c in casts],
        scratch_shapes=[pltpu.VMEM((CLS_PAD, LANES), F32)],
        compiler_params=_cparams(("arbitrary",)),
        name="outproj_route",
    )(attn, ssd, memo, x, wo, g, b, wr_t, br_col, tri, *expert_ws)


def _moe_kernel(ea_ref, eb_ref, nv_ref, tok_ref,
                x1_hbm, wga, wua, wda, wgb, wub, wdb, g_ref, b_ref, wr_ref, br_ref,
                out_hbm, xbuf0, xbuf1, obuf0, obuf1, gsem, ssem):
    i = pl.program_id(0)
    n_tiles = pl.num_programs(0)
    n_used = nv_ref[n_tiles]
    xbufs = (xbuf0, xbuf1)
    obufs = (obuf0, obuf1)
    groups = [(g * MOE_ROW_GROUP, (g + 1) * MOE_ROW_GROUP) for g in range(MOE_TM // MOE_ROW_GROUP)]

    def rows_of(tile):
        inside = jnp.logical_and(tile >= 0, tile < n_used)
        return jnp.where(inside, nv_ref[jnp.clip(tile, 0, n_tiles - 1)], 0)

    def gather_copy(tok, r, p, rows=1):
        return pltpu.make_async_copy(x1_hbm.at[pl.ds(tok, rows)], xbufs[p].at[pl.ds(r, rows)], gsem.at[p])

    def scatter_copy(dst, r, p, rows=1):
        return pltpu.make_async_copy(obufs[p].at[pl.ds(r, rows)], out_hbm.at[pl.ds(dst, rows)], ssem.at[p])

    def start_gathers(tile, n, p):
        base = jnp.maximum(tile, 0) * MOE_TM
        for lo, hi in groups:
            @pl.when(lo < n)
            def _(lo=lo, hi=hi):
                for r in range(lo, hi):
                    gather_copy(tok_ref[base + r], r, p).start()

    def wait_gathers(n, p):
        for lo, hi in groups:
            @pl.when(lo < n)
            def _(lo=lo, hi=hi):
                gather_copy(0, lo, p, rows=hi - lo).wait()

    def start_scatters(tile, n, p):
        base = jnp.maximum(tile, 0) * MOE_TM
        for lo, hi in groups:
            @pl.when(hi <= n)
            def _(lo=lo, hi=hi):
                for r in range(lo, hi):
                    scatter_copy(tok_ref[base + r], r, p).start()

        def one(r, carry):
            scatter_copy(tok_ref[base + r], r, p).start()
            return carry

        lax.fori_loop(n - n % MOE_ROW_GROUP, n, one, 0)

    def wait_scatters(n, p):
        for lo, hi in groups:
            @pl.when(hi <= n)
            def _(lo=lo, hi=hi):
                scatter_copy(0, lo, p, rows=hi - lo).wait()

        def one(r, carry):
            scatter_copy(0, 0, p).wait()
            return carry

        lax.fori_loop(n - n % MOE_ROW_GROUP, n, one, 0)

    @pl.when(i == 0)
    def _():
        for p in range(2):
            xbufs[p][...] = jnp.zeros_like(xbufs[p])
        start_gathers(0, rows_of(0), 0)

    def step(p):
        q = 1 - p
        n_cur, n_next, n_prev = rows_of(i), rows_of(i + 1), rows_of(i - 1)
        wait_gathers(n_cur, p)
        start_gathers(i + 1, n_next, q)
        start_scatters(i - 1, n_prev, q)

        x = _load_rows(xbufs[p])
        xb = x.astype(BF16)
        ea = ea_ref[i]
        eb = eb_ref[i]
        w_diff = wr_ref[pl.ds(eb, 1), :] - wr_ref[pl.ds(ea, 1), :]
        logit_diff = (jnp.sum(x * w_diff, axis=-1, keepdims=True)
                      + (br_ref[pl.ds(eb, 1), :] - br_ref[pl.ds(ea, 1), :]))
        gates = (jax.nn.sigmoid(-logit_diff), jax.nn.sigmoid(logit_diff))
        half = D_EXPERT // 2
        y = None
        for e, (wg, wu, wd) in enumerate(((wga, wua, wda), (wgb, wub, wdb))):
            acc = None
            for c in range(2):
                cs = slice(c * half, (c + 1) * half)
                gate = _dot(xb, wg[0, :, cs])
                up = _dot(xb, wu[0, :, cs])
                hid = (gate * jax.nn.sigmoid(gate) * up).astype(BF16)
                part = _dot(hid, wd[0, cs, :])
                acc = part if acc is None else acc + part
            term = gates[e] * acc
            y = term if y is None else y + term
        out = _layer_norm(ALPHA * x + y, g_ref[...], b_ref[...])
        wait_scatters(n_prev, q)
        _store_rows(obufs[p], out)

        @pl.when(i == n_used - 1)
        def _():
            start_scatters(i, n_cur, p)
            wait_scatters(n_cur, p)

    for p in range(2):
        @pl.when(jnp.logical_and(i < n_used, i % 2 == p))
        def _(p=p):
            step(p)


def _moe(x1, tok_of_slot, tile_ea, tile_eb, tile_nv, wg, wu, wd, g, b, wr_t, br_col):
    n_tok = x1.shape[0]
    nt = tile_ea.shape[0]
    wspec = lambda which: pl.BlockSpec(
        (1, D_MODEL, D_EXPERT), lambda i, ea, eb, nv, tok: ((ea if which == 0 else eb)[i], 0, 0))
    const = lambda shape: pl.BlockSpec(shape, lambda i, ea, eb, nv, tok: (0, 0))
    tile_buf = pltpu.VMEM((MOE_TM,) + TOKEN_ROW, F32)
    grid_spec = pltpu.PrefetchScalarGridSpec(
        num_scalar_prefetch=4,
        grid=(nt,),
        in_specs=[pl.BlockSpec(memory_space=pl.ANY),
                  wspec(0), wspec(0), wspec(0), wspec(1), wspec(1), wspec(1),
                  const((1, D_MODEL)), const((1, D_MODEL)), const((N_EXPERTS, D_MODEL)), const((N_EXPERTS, 1))],
        out_specs=pl.BlockSpec(memory_space=pl.ANY),
        scratch_shapes=[tile_buf, tile_buf, tile_buf, tile_buf,
                        pltpu.SemaphoreType.DMA((2,)), pltpu.SemaphoreType.DMA((2,))],
    )
    return pl.pallas_call(
        _moe_kernel,
        grid_spec=grid_spec,
        out_shape=jax.ShapeDtypeStruct((n_tok,) + TOKEN_ROW, F32),
        compiler_params=_cparams(("arbitrary",)),
        name="moe_experts",
    )(tile_ea, tile_eb, tile_nv, tok_of_slot, x1, wg, wu, wd, wg, wu, wd, g, b, wr_t, br_col)


def _routing_tables(route, counts, t):
    n_tiles = t // MOE_TM + N_CLASSES
    cls = route[0].astype(jnp.int32)
    rank = route[1].astype(jnp.int32)
    cnt = counts[:N_CLASSES, 0].astype(jnp.int32)
    tiles_per_cls = (cnt + MOE_TM - 1) // MOE_TM
    tile_end = jnp.cumsum(tiles_per_cls)
    tile_start = tile_end - tiles_per_cls
    dest = tile_start[cls] * MOE_TM + rank
    tok_of_slot = jnp.zeros((n_tiles * MOE_TM,), jnp.int32).at[dest].set(jnp.arange(t, dtype=jnp.int32))
    tile_id = jnp.arange(n_tiles, dtype=jnp.int32)
    n_used = tile_end[-1]
    tile_cls = jnp.minimum(jnp.searchsorted(tile_end, jnp.minimum(tile_id, n_used - 1), side="right"),
                           N_CLASSES - 1).astype(jnp.int32)
    tile_nv = jnp.where(tile_id < n_used,
                        jnp.clip(cnt[tile_cls] - (tile_id - tile_start[tile_cls]) * MOE_TM, 0, MOE_TM), 0)
    pair_lo = jnp.asarray([p[0] for p in PAIRS], jnp.int32)
    pair_hi = jnp.asarray([p[1] for p in PAIRS], jnp.int32)
    grp_base = (tile_cls // len(PAIRS)) * EXPERTS_PER_GROUP
    tile_ea = grp_base + pair_lo[tile_cls % len(PAIRS)]
    tile_eb = grp_base + pair_hi[tile_cls % len(PAIRS)]
    return tok_of_slot, tile_ea, tile_eb, jnp.concatenate([tile_nv, n_used[None]]).astype(jnp.int32)


def _rearranged_w_in(w):
    scale = HEAD_DIM ** -0.5
    s = np.cumsum([ATTN_WIDTH, ATTN_WIDTH, ATTN_WIDTH, SSD_WIDTH, CONV_CH, N_SSD_HEADS]).tolist()
    q, k, v, z, xbc, dt, qm = (w[:, a:b] for a, b in zip([0] + s, s + [w.shape[1]]))
    dt = jnp.pad(dt, ((0, 0), (0, LANES - N_SSD_HEADS)))
    return jnp.concatenate([q * scale, k, v, z, xbc, qm * scale, dt], axis=1).astype(BF16)


def kernel(x, mem, w_in, w_out, rel_bias, conv_w, conv_b, dt_bias, a_log, d_skip, ssd_norm_w, w_mem_kv, mem_bias,
           ln1_g, ln1_b, ln2_g, ln2_b, w_router, b_router, w_gate, w_up, w_down):
    batch, seq, _ = x.shape
    t = batch * seq
    assert seq % (16 * ATTN_BLK) == 0 and seq % SSD_TM == 0 and t % MOE_TM == 0
    bias = _attn_bias_tables(rel_bias)
    tri = jnp.asarray(np.triu(np.ones((OUT_TM, OUT_TM), np.float32), k=1), BF16)
    wr_t = w_router.astype(F32).T
    br_col = b_router.astype(F32)[:, None]
    h = x.reshape(t, D_MODEL)
    for l in range(DEPTH):
        q, k, v, z, xbc, qm, dt, wg = _inproj(h, _rearranged_w_in(w_in[l]), t, w_gate, l)
        attn = _attention(q, k, v, bias, batch, seq)
        ssd = _ssd(z, xbc, dt, conv_w[l], conv_b[l], dt_bias[l], a_log[l], d_skip[l], ssd_norm_w[l], batch, seq)
        memo = _memory_attention(qm, mem, w_mem_kv[l].astype(BF16), mem_bias[l], batch, seq)
        x1, route, counts, wu, wd = _outproj_route(attn, ssd, memo, h, w_out[l].astype(BF16),
                                                   ln1_g[l][None, :], ln1_b[l][None, :], wr_t, br_col, tri, t,
                                                   (w_up, w_down), l)
        tok_of_slot, tile_ea, tile_eb, tile_nv = _routing_tables(route, counts, t)
        h = _moe(x1, tok_of_slot, tile_ea, tile_eb, tile_nv,
                 wg, wu, wd, ln2_g[l][None, :], ln2_b[l][None, :], wr_t, br_col)
    return h.reshape(batch, seq, D_MODEL)
```
